```python
import math
import jax, jax.numpy as jnp
from jax import lax
import numpy as np

D_MODEL = 1024
BATCH = 2
SEQ = 8192
DEPTH = 1
DEC_BATCH = 128
DEC_SEQ = 1
PAST_LEN = 16384
PAGE_SIZE = 128

N_META = 16
H_A = 8
D_NOPE = 64
D_ROPE = 32
D_VA = 64
D_C = 256
D_CQ = 384
H_B = 4
D_B = 64
MIX_A = H_A * D_VA
MIX_B = H_B * 2 * D_B
MIX = MIX_A + MIX_B
IN_COLS = D_CQ + D_C + D_ROPE + 3 * MIX_B
D_FF = 4 * D_MODEL
N_BUCKETS = 32
MAX_DISTANCE = 128
ROPE_BASE = 10000.0
Q_BLOCK = 128
EPS = 1e-6

kernel_name = "hymba_mla_diffattn_decode_step"


def rmsnorm(x, g):
    xf = x.astype(jnp.float32)
    y = xf * lax.rsqrt(jnp.mean(xf * xf, axis=-1, keepdims=True) + EPS)
    return (y * g.astype(jnp.float32)).astype(x.dtype)


def rope(x, pos):
    half = x.shape[-1] // 2
    inv = ROPE_BASE ** (-jnp.arange(half, dtype=jnp.float32) / half)
    ang = pos.astype(jnp.float32)[:, None] * inv[None, :]
    ang = ang.reshape((ang.shape[0],) + (1,) * (x.ndim - 3) + (half,))
    cos, sin = jnp.cos(ang), jnp.sin(ang)
    xf = x.astype(jnp.float32)
    x1, x2 = xf[..., :half], xf[..., half:]
    return jnp.concatenate([x1 * cos - x2 * sin, x1 * sin + x2 * cos], axis=-1).astype(x.dtype)


def t5_bucket(dist):
    n = jnp.maximum(dist, 0)
    max_exact = N_BUCKETS // 2
    nf = jnp.maximum(n, max_exact).astype(jnp.float32)
    large = max_exact + (jnp.log(nf / max_exact) / math.log(MAX_DISTANCE / max_exact)
                         * (N_BUCKETS - max_exact)).astype(jnp.int32)
    large = jnp.minimum(large, N_BUCKETS - 1)
    return jnp.where(n < max_exact, n, large)


def causal_mask(q_pos, k_pos):
    return k_pos[None, :] <= q_pos[:, None]


def mla_attend(q_lat, q_rope, q_pos, segments):
    scale = (D_NOPE + D_ROPE) ** -0.5
    logits = []
    for c_kv, k_rope, k_pos in segments:
        s = jnp.einsum("bthc,bsc->bhts", q_lat, c_kv) + jnp.einsum("bthr,bsr->bhts", q_rope, k_rope)
        s = s.astype(jnp.float32) * scale
        logits.append(jnp.where(causal_mask(q_pos, k_pos), s, -jnp.inf))
    p = jax.nn.softmax(jnp.concatenate(logits, axis=-1), axis=-1)
    out, off = None, 0
    for c_kv, _, k_pos in segments:
        n = k_pos.shape[0]
        o = jnp.einsum("bhts,bsc->bthc", p[..., off:off + n].astype(c_kv.dtype), c_kv)
        out = o if out is None else out + o
        off += n
    return out


def diff_attend(dq, q_pos, segments, rel_bias, lam):
    scale = D_B ** -0.5
    logits = []
    for k, v, k_pos in segments:
        s = jnp.einsum("btham,bsham->bhats", dq, k).astype(jnp.float32) * scale
        bias = rel_bias[t5_bucket(q_pos[:, None] - k_pos[None, :])].astype(jnp.float32)
        s = s + jnp.transpose(bias, (2, 0, 1))[None, :, None]
        logits.append(jnp.where(causal_mask(q_pos, k_pos), s, -jnp.inf))
    p = jax.nn.softmax(jnp.concatenate(logits, axis=-1), axis=-1)
    a = p[:, :, 0] - lam * p[:, :, 1]
    out, off = None, 0
    for _, v, k_pos in segments:
        n = k_pos.shape[0]
        o = jnp.einsum("bhts,bshv->bthv", a[..., off:off + n].astype(v.dtype), v)
        out = o if out is None else out + o
        off += n
    return out


def sweep_queries(fn, q_arrays, q_pos, n_lead):
    lead = fn(tuple(a[:, :n_lead] for a in q_arrays), q_pos[:n_lead])
    b, length = q_arrays[0].shape[0], q_arrays[0].shape[1]
    nb = (length - n_lead) // Q_BLOCK
    blocks = tuple(jnp.moveaxis(a[:, n_lead:].reshape((b, nb, Q_BLOCK) + a.shape[2:]), 1, 0) for a in q_arrays)
    pos_blocks = q_pos[n_lead:].reshape(nb, Q_BLOCK)
    rest = lax.map(lambda args: fn(args[0], args[1]), (blocks, pos_blocks))
    rest = jnp.moveaxis(rest, 0, 1).reshape((b, nb * Q_BLOCK) + rest.shape[3:])
    return jnp.concatenate([lead, rest], axis=1)


def trunk_layer(x, pos, past, lam_init, rel_bias, norm1, w_in, q_norm, w_uq, kv_norm, w_uk, w_uv,
                lambda_q1, lambda_k1, lambda_q2, lambda_k2, subln, w_o, norm2, w_up, w_down):
    b, t, _ = x.shape
    h = rmsnorm(x, norm1)
    z = h @ w_in
    cuts = [D_CQ, D_CQ + D_C, D_CQ + D_C + D_ROPE, D_CQ + D_C + D_ROPE + MIX_B, D_CQ + D_C + D_ROPE + 2 * MIX_B]
    c_q, c_kv, k_rope, dq, dk, dv = jnp.split(z, cuts, axis=-1)
    qa = (rmsnorm(c_q, q_norm) @ w_uq).reshape(b, t, H_A, D_NOPE + D_ROPE)
    q_lat = jnp.einsum("bthd,chd->bthc", qa[..., :D_NOPE], w_uk)
    q_rope = rope(qa[..., D_NOPE:], pos)
    c_kv = rmsnorm(c_kv, kv_norm)
    k_rope = rope(k_rope, pos)
    dq = dq.reshape(b, t, H_B, 2, D_B)
    dk = dk.reshape(b, t, H_B, 2, D_B)
    dv = dv.reshape(b, t, H_B, 2 * D_B)
    lam = (jnp.exp(jnp.sum(lambda_q1.astype(jnp.float32) * lambda_k1.astype(jnp.float32)))
           - jnp.exp(jnp.sum(lambda_q2.astype(jnp.float32) * lambda_k2.astype(jnp.float32))) + lam_init)
    mla_segs = ((c_kv, k_rope, pos),)
    diff_segs = ((dk, dv, pos),)
    if past is not None:
        p_c, p_kr, p_k, p_v, p_pos = past
        mla_segs = ((p_c, p_kr, p_pos),) + mla_segs
        diff_segs = ((p_k, p_v, p_pos),) + diff_segs
    mla_fn = lambda qs, qp: mla_attend(qs[0], qs[1], qp, mla_segs)
    diff_fn = lambda qs, qp: diff_attend(qs[0], qp, diff_segs, rel_bias, lam)
    if past is None:
        o_lat = sweep_queries(mla_fn, (q_lat, q_rope), pos, N_META)
        o_b = sweep_queries(diff_fn, (dq,), pos, N_META)
    else:
        o_lat = mla_fn((q_lat, q_rope), pos)
        o_b = diff_fn((dq,), pos)
    o_a = jnp.einsum("bthc,chv->bthv", o_lat, w_uv).reshape(b, t, MIX_A)
    o_b = (rmsnorm(o_b, subln) * (1.0 - lam_init)).reshape(b, t, MIX_B)
    x = x + jnp.concatenate([o_a, o_b], axis=-1) @ w_o
    h2 = rmsnorm(x, norm2)
    x = x + jnp.square(jax.nn.relu(h2 @ w_up)) @ w_down
    return x, (c_kv, k_rope, dk, dv)


def setup_inputs(seed: int = 0) -> dict:
    key = jax.random.key(seed)
    ks = jax.random.split(key, 26)
    n_pages = PAST_LEN // PAGE_SIZE
    n_used = DEC_BATCH * n_pages
    n_phys = n_used + n_used // 4
    nrm = lambda k, shape, s: jax.random.normal(k, shape, jnp.float32) * s
    gain = lambda k, shape: 1.0 + 0.02 * jax.random.normal(k, shape, jnp.float32)
    page_table = jax.random.permutation(ks[6], n_phys)[:n_used].reshape(DEC_BATCH, n_pages).astype(jnp.int32)
    return {
        "x_prompt": nrm(ks[0], (BATCH, SEQ, D_MODEL), 1.0),
        "x_sample": nrm(ks[1], (DEC_BATCH, DEC_SEQ, D_MODEL), 1.0),
        "cache_latent": nrm(ks[2], (DEPTH, n_phys, PAGE_SIZE, D_C), 1.0),
        "cache_krope": nrm(ks[3], (DEPTH, n_phys, PAGE_SIZE, D_ROPE), 1.0),
        "cache_diff_k": nrm(ks[4], (DEPTH, n_phys, PAGE_SIZE, H_B, 2, D_B), 1.0),
        "cache_diff_v": nrm(ks[5], (DEPTH, n_phys, PAGE_SIZE, H_B, 2 * D_B), 1.0),
        "page_table": page_table,
        "meta_tokens": nrm(ks[7], (N_META, D_MODEL), 1.0),
        "rel_bias": nrm(ks[8], (N_BUCKETS, H_B), 0.5),
        "norm1": gain(ks[9], (DEPTH, D_MODEL)),
        "w_in": nrm(ks[10], (DEPTH, D_MODEL, IN_COLS), D_MODEL ** -0.5),
        "q_norm": gain(ks[11], (DEPTH, D_CQ)),
        "w_uq": nrm(ks[12], (DEPTH, D_CQ, H_A * (D_NOPE + D_ROPE)), D_CQ ** -0.5),
        "kv_norm": gain(ks[13], (DEPTH, D_C)),
        "w_uk": nrm(ks[14], (DEPTH, D_C, H_A, D_NOPE), D_C ** -0.5),
        "w_uv": nrm(ks[15], (DEPTH, D_C, H_A, D_VA), D_C ** -0.5),
        "lambda_q1": nrm(ks[16], (DEPTH, D_B), 0.1),
        "lambda_k1": nrm(ks[17], (DEPTH, D_B), 0.1),
        "lambda_q2": nrm(ks[18], (DEPTH, D_B), 0.1),
        "lambda_k2": nrm(ks[19], (DEPTH, D_B), 0.1),
        "subln": gain(ks[20], (DEPTH, 2 * D_B)),
        "w_o": nrm(ks[21], (DEPTH, MIX, D_MODEL), MIX ** -0.5),
        "norm2": gain(ks[22], (DEPTH, D_MODEL)),
        "w_up": nrm(ks[23], (DEPTH, D_MODEL, D_FF), D_MODEL ** -0.5),
        "w_down": nrm(ks[24], (DEPTH, D_FF, D_MODEL), D_FF ** -0.5),
        "final_norm": gain(ks[25], (D_MODEL,)),
    }


def reference(x_prompt, x_sample, cache_latent, cache_krope, cache_diff_k, cache_diff_v, page_table,
              meta_tokens, rel_bias, norm1, w_in, q_norm, w_uq, kv_norm, w_uk, w_uv,
              lambda_q1, lambda_k1, lambda_q2, lambda_k2, subln, w_o, norm2, w_up, w_down, final_norm):
    b = x_prompt.shape[0]
    meta = jnp.broadcast_to(meta_tokens[None].astype(x_prompt.dtype), (b, N_META, D_MODEL))
    xp = jnp.concatenate([meta, x_prompt], axis=1)
    xs = x_sample
    past_len = page_table.shape[1] * cache_latent.shape[2]
    pos_p = jnp.arange(xp.shape[1], dtype=jnp.int32)
    pos_s = past_len + jnp.arange(xs.shape[1], dtype=jnp.int32)
    past_pos = jnp.arange(past_len, dtype=jnp.int32)

    def gather(pool, l):
        g = pool[l, page_table]
        return g.reshape((g.shape[0], g.shape[1] * g.shape[2]) + g.shape[3:])

    rows_p, rows_s = [], []
    for l in range(DEPTH):
        lam_init = 0.8 - 0.6 * math.exp(-0.3 * l)
        weights = (norm1[l], w_in[l], q_norm[l], w_uq[l], kv_norm[l], w_uk[l], w_uv[l],
                   lambda_q1[l], lambda_k1[l], lambda_q2[l], lambda_k2[l], subln[l], w_o[l], norm2[l], w_up[l], w_down[l])
        past = (gather(cache_latent, l), gather(cache_krope, l), gather(cache_diff_k, l), gather(cache_diff_v, l), past_pos)
        xp, rp = trunk_layer(xp, pos_p, None, lam_init, rel_bias, *weights)
        xs, rs = trunk_layer(xs, pos_s, past, lam_init, rel_bias, *weights)
        rows_p.append(rp)
        rows_s.append(rs)

    y_prompt = rmsnorm(xp[:, N_META:], final_norm)
    y_sample = rmsnorm(xs, final_norm)
    stk = lambda rows, i: jnp.stack([r[i] for r in rows], axis=0)
    return (y_prompt, y_sample,
            stk(rows_p, 0), stk(rows_p, 1), stk(rows_p, 2), stk(rows_p, 3),
            stk(rows_s, 0), stk(rows_s, 1), stk(rows_s, 2), stk(rows_s, 3))
```

```python
import functools
import math

import numpy as np
import jax
import jax.numpy as jnp
from jax import lax
from jax.experimental import pallas as pl
from jax.experimental.pallas import tpu as pltpu

N_META = 16
H_A = 8
D_NOPE = 64
D_ROPE = 32
D_VA = 64
D_C = 256
D_CQ = 384
H_B = 4
D_B = 64
MIX_A = H_A * D_VA
MIX_B = H_B * 2 * D_B
N_BUCKETS = 32
MAX_DISTANCE = 128
ROPE_BASE = 10000.0
EPS = 1e-6

LOG2E = math.log2(math.e)
SCALE_A = (D_NOPE + D_ROPE) ** -0.5 * LOG2E
SCALE_B = D_B ** -0.5 * LOG2E

LANES = 128
D_QK = D_C + LANES
ATT_TILE = 256
ROW_TILE = 512
PAGES_PER_STEP = 8
VMEM_LIMIT = 56 * 1024 * 1024

_C_Q = 0
_C_KV = _C_Q + D_CQ
_C_DQ = _C_KV + D_C
_C_DK = _C_DQ + MIX_B
_C_DV = _C_DK + MIX_B
_C_KR = _C_DV + MIX_B
_C_KRR = _C_KR + LANES
_IN_COLS_P = _C_KRR + LANES

F32 = jnp.float32
BF16 = jnp.bfloat16
NEG_INF = float("-inf")


def _rms(x, g):
    return x * lax.rsqrt(jnp.mean(x * x, axis=-1, keepdims=True) + EPS) * g


def _dot(a, b):
    return jnp.dot(a, b, preferred_element_type=F32)


def _dot_nt(a, b):
    return lax.dot_general(a, b, (((1,), (1,)), ((), ())), preferred_element_type=F32)


def _bucket_np(dist):
    n = np.maximum(dist, 0)
    max_exact = N_BUCKETS // 2
    nf = np.maximum(n, max_exact).astype(np.float32)
    large = max_exact + (np.log(nf / np.float32(max_exact)) / np.float32(math.log(MAX_DISTANCE / max_exact))
                         * np.float32(N_BUCKETS - max_exact)).astype(np.int32)
    large = np.minimum(large, N_BUCKETS - 1)
    return np.where(n < max_exact, n, large).astype(np.int32)


def _proj_kernel(x_ref, n1_ref, win_ref, qn_ref, wuq_ref, kvn_ref, wuk_ref, ck_ref, sk_ref, cq_ref, sq_ref,
                 ckv_ref, kr_ref, dk_ref, dv_ref, q_ref, dqb_ref, kb_ref, dkb_ref, dvb_ref):
    h = _rms(x_ref[...], n1_ref[...]).astype(BF16)
    z = _dot(h, win_ref[...])
    c_kv = _rms(z[:, _C_KV:_C_KV + D_C], kvn_ref[...])
    ckv_ref[...] = c_kv
    k_rope = z[:, _C_KR:_C_KR + LANES] * ck_ref[...] + z[:, _C_KRR:_C_KRR + LANES] * sk_ref[...]
    kr_ref[...] = k_rope[:, :D_ROPE]
    kb_ref[...] = jnp.concatenate([c_kv, k_rope], axis=1).astype(BF16)
    dk = z[:, _C_DK:_C_DK + MIX_B]
    dk_ref[...] = dk
    dkb_ref[...] = dk.astype(BF16)
    dv = z[:, _C_DV:_C_DV + MIX_B]
    dv_ref[...] = dv
    dvb_ref[...] = dv.astype(BF16)
    dqb_ref[...] = (z[:, _C_DQ:_C_DQ + MIX_B] * SCALE_B).astype(BF16)
    c_q = _rms(z[:, _C_Q:_C_Q + D_CQ], qn_ref[...]).astype(BF16)
    qa = _dot(c_q, wuq_ref[...])
    n_nope = H_A * D_NOPE
    n_rope = H_A * D_ROPE
    q_rope = (qa[:, n_nope:n_nope + n_rope] * cq_ref[...] + qa[:, n_nope + n_rope:] * sq_ref[...]) * SCALE_A
    lane = lax.broadcasted_iota(jnp.int32, (q_rope.shape[0], LANES), 1)
    heads_per_tile = LANES // D_ROPE
    for pair in range(H_A // 2):
        qn2 = qa[:, pair * 2 * D_NOPE:(pair + 1) * 2 * D_NOPE].astype(BF16)
        ql2 = _dot(qn2, wuk_ref[pair]) * SCALE_A
        for j in range(2):
            hd = 2 * pair + j
            tile_r = q_rope[:, (hd // heads_per_tile) * LANES:(hd // heads_per_tile + 1) * LANES]
            shift = (LANES - (hd % heads_per_tile) * D_ROPE) % LANES
            if shift:
                tile_r = pltpu.roll(tile_r, shift, 1)
            qr128 = jnp.where(lane < D_ROPE, tile_r, 0.0)
            q_ref[hd] = jnp.concatenate([ql2[:, j * D_C:(j + 1) * D_C], qr128], axis=1).astype(BF16)


def _proj_call(x2d, tables, n_pos_tiles, w):
    n, d_model = x2d.shape
    r = min(ROW_TILE, n)
    assert n % r == 0
    steps = n // r
    ck, sk, cq, sq = tables
    row = lambda i: (i, 0)
    pos = lambda i: (i % n_pos_tiles, 0)
    const2 = lambda i: (0, 0)
    const3 = lambda i: (0, 0, 0)
    head = lambda i: (0, i, 0)
    in_specs = [
        pl.BlockSpec((r, d_model), row),
        pl.BlockSpec((1, d_model), const2),
        pl.BlockSpec(w["w_in"].shape, const2),
        pl.BlockSpec((1, D_CQ), const2),
        pl.BlockSpec(w["w_uq"].shape, const2),
        pl.BlockSpec((1, D_C), const2),
        pl.BlockSpec(w["w_uk2"].shape, const3),
        pl.BlockSpec((r, LANES), pos),
        pl.BlockSpec((r, LANES), pos),
        pl.BlockSpec((r, H_A * D_ROPE), pos),
        pl.BlockSpec((r, H_A * D_ROPE), pos),
    ]
    out_shape = [
        jax.ShapeDtypeStruct((n, D_C), F32),
        jax.ShapeDtypeStruct((n, D_ROPE), F32),
        jax.ShapeDtypeStruct((n, MIX_B), F32),
        jax.ShapeDtypeStruct((n, MIX_B), F32),
        jax.ShapeDtypeStruct((H_A, n, D_QK), BF16),
        jax.ShapeDtypeStruct((n, MIX_B), BF16),
        jax.ShapeDtypeStruct((n, D_QK), BF16),
        jax.ShapeDtypeStruct((n, MIX_B), BF16),
        jax.ShapeDtypeStruct((n, MIX_B), BF16),
    ]
    out_specs = [
        pl.BlockSpec((r, D_C), row),
        pl.BlockSpec((r, D_ROPE), row),
        pl.BlockSpec((r, MIX_B), row),
        pl.BlockSpec((r, MIX_B), row),
        pl.BlockSpec((H_A, r, D_QK), head),
        pl.BlockSpec((r, MIX_B), row),
        pl.BlockSpec((r, D_QK), row),
        pl.BlockSpec((r, MIX_B), row),
        pl.BlockSpec((r, MIX_B), row),
    ]
    return pl.pallas_call(
        _proj_kernel,
        grid=(steps,),
        in_specs=in_specs,
        out_specs=out_specs,
        out_shape=out_shape,
        compiler_params=pltpu.CompilerParams(dimension_semantics=("arbitrary",), vmem_limit_bytes=VMEM_LIMIT),
        name="proj",
    )(x2d, w["norm1"], w["w_in"], w["q_norm"], w["w_uq"], w["kv_norm"], w["w_uk2"], ck, sk, cq, sq)


def _softmax_init(s, v, m_ref, l_ref, acc_ref):
    m = jnp.max(s, axis=1, keepdims=True)
    p = jnp.exp2(s - m)
    m_ref[...] = m
    l_ref[...] = jnp.sum(p, axis=1, keepdims=True)
    acc_ref[...] = _dot(p.astype(BF16), v)


def _softmax_step(s, v, m_ref, l_ref, acc_ref):
    m_prev = m_ref[...]
    m_new = jnp.maximum(m_prev, jnp.max(s, axis=1, keepdims=True))
    alpha = jnp.exp2(m_prev - m_new)
    p = jnp.exp2(s - m_new)
    l_ref[...] = alpha * l_ref[...] + jnp.sum(p, axis=1, keepdims=True)
    acc_ref[...] = alpha * acc_ref[...] + _dot(p.astype(BF16), v)
    m_ref[...] = m_new


def _mla_kernel(q_ref, k_ref, wuv_ref, o_ref, m_ref, l_ref, acc_ref, *, seq, tile):
    i = pl.program_id(1)
    rows = H_A * tile
    q = q_ref[...].reshape(rows, D_QK)

    def scores(start, size):
        k = k_ref[pl.ds(start, size), :]
        return _dot_nt(q, k), k[:, :D_C]

    s, kl = scores(seq, LANES)
    col = lax.broadcasted_iota(jnp.int32, (rows, LANES), 1)
    _softmax_init(jnp.where(col < N_META, s, NEG_INF), kl, m_ref, l_ref, acc_ref)

    def body(kt, carry):
        s, kl = scores(pl.multiple_of(kt * tile, tile), tile)
        _softmax_step(s, kl, m_ref, l_ref, acc_ref)
        return carry

    lax.fori_loop(0, i, body, 0)

    s, kl = scores(pl.multiple_of(i * tile, tile), tile)
    r_in = lax.broadcasted_iota(jnp.int32, (H_A, tile, tile), 1).reshape(rows, tile)
    c_in = lax.broadcasted_iota(jnp.int32, (rows, tile), 1)
    _softmax_step(jnp.where(c_in <= r_in, s, NEG_INF), kl, m_ref, l_ref, acc_ref)

    o_lat = (acc_ref[...] / l_ref[...]).astype(BF16)
    o_ref[...] = jnp.concatenate(
        [_dot(o_lat[hd * tile:(hd + 1) * tile], wuv_ref[hd]) for hd in range(H_A)], axis=1).astype(BF16)


def _mla_call(q, k, wuv, seq):
    b = k.shape[0]
    tile = ATT_TILE
    assert seq % tile == 0 and tile >= MAX_DISTANCE
    nk = k.shape[1]
    rows = H_A * tile
    nq = seq // tile
    return pl.pallas_call(
        functools.partial(_mla_kernel, seq=seq, tile=tile),
        grid=(b, nq),
        in_specs=[
            pl.BlockSpec((H_A, tile, D_QK), lambda bi, i: (0, bi * nq + i, 0)),
            pl.BlockSpec((None, nk, D_QK), lambda bi, i: (bi, 0, 0)),
            pl.BlockSpec(wuv.shape, lambda bi, i: (0, 0, 0)),
        ],
        out_specs=pl.BlockSpec((None, tile, MIX_A), lambda bi, i: (bi, i, 0)),
        out_shape=jax.ShapeDtypeStruct((b, seq, MIX_A), BF16),
        scratch_shapes=[pltpu.VMEM((rows, 1), F32), pltpu.VMEM((rows, 1), F32), pltpu.VMEM((rows, D_C), F32)],
        compiler_params=pltpu.CompilerParams(dimension_semantics=("arbitrary", "arbitrary"),
                                             vmem_limit_bytes=VMEM_LIMIT),
        name="mla",
    )(q, k, wuv)


def _lambda_value(lq1_ref, lk1_ref, lq2_ref, lk2_ref, lam_init):
    s1 = jnp.sum(lq1_ref[...] * lk1_ref[...], axis=1, keepdims=True)
    s2 = jnp.sum(lq2_ref[...] * lk2_ref[...], axis=1, keepdims=True)
    return jnp.exp(s1) - jnp.exp(s2) + lam_init


def _bias_from_buckets(bk, relb_ref, hd):
    far = relb_ref[N_BUCKETS - 1, hd]
    bias = jnp.zeros(bk.shape, F32)
    for bkt in range(N_BUCKETS - 1):
        bias = jnp.where(bk == bkt, (relb_ref[bkt, hd] - far) * LOG2E, bias)
    return bias


def _diff_kernel(relb_ref, q_ref, k_ref, v_ref, bkd_ref, bkp_ref, bkm_ref, lq1_ref, lk1_ref, lq2_ref, lk2_ref,
                 subln_ref, o_ref, bd_ref, bp_ref, bm_ref, m_ref, l_ref, acc_ref, *, seq, tile, lam_init):
    hd = pl.program_id(1)
    i = pl.program_id(2)
    width = 2 * D_B

    @pl.when(i == 0)
    def _():
        bd = _bias_from_buckets(bkd_ref[...], relb_ref, hd)
        bd_ref[...] = jnp.concatenate([bd, bd], axis=0)
        bp = _bias_from_buckets(bkp_ref[...], relb_ref, hd)
        bp_ref[...] = jnp.concatenate([bp, bp], axis=0)
        bm = _bias_from_buckets(bkm_ref[...], relb_ref, hd)
        bm_ref[...] = jnp.concatenate([bm, bm], axis=0)

    q = q_ref[...]
    lane = lax.broadcasted_iota(jnp.int32, (tile, width), 1)
    zero = jnp.zeros_like(q)
    qbd = jnp.concatenate([jnp.where(lane < D_B, q, zero), jnp.where(lane >= D_B, q, zero)], axis=0)

    def scores(start, size):
        return _dot_nt(qbd, k_ref[pl.ds(start, size), :]), v_ref[pl.ds(start, size), :]

    s, v = scores(seq, LANES)
    s = s + jnp.where(i == 0, 1.0, 0.0) * bm_ref[...]
    col = lax.broadcasted_iota(jnp.int32, (2 * tile, LANES), 1)
    _softmax_init(jnp.where(col < N_META, s, NEG_INF), v, m_ref, l_ref, acc_ref)

    def body(kt, carry):
        s, v = scores(pl.multiple_of(kt * tile, tile), tile)
        _softmax_step(s, v, m_ref, l_ref, acc_ref)
        return carry

    lax.fori_loop(0, i - 1, body, 0)

    @pl.when(i > 0)
    def _():
        s, v = scores(pl.multiple_of((i - 1) * tile, tile), tile)
        _softmax_step(s + bp_ref[...], v, m_ref, l_ref, acc_ref)

    s, v = scores(pl.multiple_of(i * tile, tile), tile)
    r_in = lax.broadcasted_iota(jnp.int32, (2, tile, tile), 1).reshape(2 * tile, tile)
    c_in = lax.broadcasted_iota(jnp.int32, (2 * tile, tile), 1)
    _softmax_step(jnp.where(c_in <= r_in, s + bd_ref[...], NEG_INF), v, m_ref, l_ref, acc_ref)

    o = acc_ref[...] / l_ref[...]
    lam = _lambda_value(lq1_ref, lk1_ref, lq2_ref, lk2_ref, lam_init)
    a = o[:tile] - lam * o[tile:]
    o_ref[...] = (_rms(a, subln_ref[...]) * (1.0 - lam_init)).astype(BF16)


def _diff_call(relb, dq, dk, dv, lam_w, subln, seq, lam_init):
    b = dq.shape[0]
    tile = ATT_TILE
    assert seq % tile == 0 and tile >= MAX_DISTANCE
    nk = dk.shape[1]
    width = 2 * D_B
    r = np.arange(tile)[:, None]
    c = np.arange(tile)[None, :]
    bk_diag = jnp.asarray(_bucket_np(r - c))
    bk_prev = jnp.asarray(_bucket_np(tile + r - c))
    bk_meta = jnp.asarray(_bucket_np(N_META + r - np.arange(LANES)[None, :]))
    const = lambda bi, h, i: (0, 0)
    smem = pl.BlockSpec(memory_space=pltpu.SMEM)
    return pl.pallas_call(
        functools.partial(_diff_kernel, seq=seq, tile=tile, lam_init=lam_init),
        grid=(b, H_B, seq // tile),
        in_specs=[
            smem,
            pl.BlockSpec((None, tile, width), lambda bi, h, i: (bi, i, h)),
            pl.BlockSpec((None, nk, width), lambda bi, h, i: (bi, 0, h)),
            pl.BlockSpec((None, nk, width), lambda bi, h, i: (bi, 0, h)),
            pl.BlockSpec((tile, tile), const),
            pl.BlockSpec((tile, tile), const),
            pl.BlockSpec((tile, LANES), const),
            pl.BlockSpec((1, LANES), const),
            pl.BlockSpec((1, LANES), const),
            pl.BlockSpec((1, LANES), const),
            pl.BlockSpec((1, LANES), const),
            pl.BlockSpec((1, width), const),
        ],
        out_specs=pl.BlockSpec((None, tile, width), lambda bi, h, i: (bi, i, h)),
        out_shape=jax.ShapeDtypeStruct((b, seq, MIX_B), BF16),
        scratch_shapes=[
            pltpu.VMEM((2 * tile, tile), F32),
            pltpu.VMEM((2 * tile, tile), F32),
            pltpu.VMEM((2 * tile, LANES), F32),
            pltpu.VMEM((2 * tile, 1), F32),
            pltpu.VMEM((2 * tile, 1), F32),
            pltpu.VMEM((2 * tile, width), F32),
        ],
        compiler_params=pltpu.CompilerParams(dimension_semantics=("arbitrary", "arbitrary", "arbitrary"),
                                             vmem_limit_bytes=VMEM_LIMIT),
        name="diff",
    )(relb, dq, dk, dv, bk_diag, bk_prev, bk_meta, *lam_w, subln)


def _decode_kernel(pt_ref, *refs, n_pages, group, lam_init):
    del pt_ref
    page_refs = refs[:4 * group]
    (q_ref, dq_ref, kself_ref, dk_ref, dv_ref, relb_ref, bkl_ref, wuv_ref,
     lq1_ref, lk1_ref, lq2_ref, lk2_ref, subln_ref,
     oa_ref, ob_ref, ma_ref, la_ref, acca_ref, md_ref, ld_ref, accd_ref) = refs[4 * group:]
    lat_refs = page_refs[0::4]
    krt_refs = page_refs[1::4]
    dkt_refs = page_refs[2::4]
    dvp_refs = page_refs[3::4]
    c = pl.program_id(1)
    n_steps = n_pages // group
    n_maps = 2 * H_B

    q = q_ref[...]
    ql = q[:, :D_C]
    qr = q[:, D_C:D_C + D_ROPE]
    row = lax.broadcasted_iota(jnp.int32, (n_maps, MIX_B), 0)
    lane_blk = lax.broadcasted_iota(jnp.int32, (n_maps, MIX_B), 1) // D_B
    dqbd = jnp.where(row == lane_blk, jnp.broadcast_to(dq_ref[...], (n_maps, MIX_B)), 0.0)
    relb = relb_ref[...]
    far = relb[:, N_BUCKETS - 1:N_BUCKETS]

    @pl.when(c == 0)
    def _():
        k_self = kself_ref[...]
        ckv = k_self[:, :D_C]
        ma_ref[...] = jnp.sum(q * k_self, axis=1, keepdims=True)
        la_ref[...] = jnp.ones_like(la_ref)
        acca_ref[...] = jnp.broadcast_to(ckv, (H_A, D_C))
        md_ref[...] = jnp.sum(dqbd * dk_ref[...], axis=1, keepdims=True) + (relb[:, 0:1] - far) * LOG2E
        ld_ref[...] = jnp.ones_like(ld_ref)
        dv = dv_ref[...]
        accd_ref[...] = jnp.concatenate(
            [jnp.broadcast_to(dv[:, (r // 2) * 2 * D_B:(r // 2 + 1) * 2 * D_B], (1, 2 * D_B)) for r in range(n_maps)], axis=0)

    s_a = [_dot_nt(ql, lat_refs[g][...]) + _dot(qr, krt_refs[g][...]) for g in range(group)]
    m_prev = ma_ref[...]
    m_new = m_prev
    for s in s_a:
        m_new = jnp.maximum(m_new, jnp.max(s, axis=1, keepdims=True))
    alpha = jnp.exp2(m_prev - m_new)
    l_new = alpha * la_ref[...]
    pv = None
    for g in range(group):
        p = jnp.exp2(s_a[g] - m_new)
        l_new = l_new + jnp.sum(p, axis=1, keepdims=True)
        part = _dot(p, lat_refs[g][...])
        pv = part if pv is None else pv + part
    ma_ref[...] = m_new
    la_ref[...] = l_new
    acca_ref[...] = alpha * acca_ref[...] + pv

    bkl = bkl_ref[...]
    bias_last = jnp.zeros((n_maps, bkl.shape[1]), F32)
    for bkt in range(N_BUCKETS - 1):
        bias_last = jnp.where(bkl == bkt, (relb[:, bkt:bkt + 1] - far) * LOG2E, bias_last)
    s_d = [_dot(dqbd, dkt_refs[g][...]) for g in range(group)]
    s_d[group - 1] = s_d[group - 1] + jnp.where(c == n_steps - 1, 1.0, 0.0) * bias_last
    m_prev = md_ref[...]
    m_new = m_prev
    for s in s_d:
        m_new = jnp.maximum(m_new, jnp.max(s, axis=1, keepdims=True))
    alpha = jnp.exp2(m_prev - m_new)
    l_new = alpha * ld_ref[...]
    page = bkl.shape[1]
    pv = None
    for g in range(group):
        p = jnp.exp2(s_d[g] - m_new)
        l_new = l_new + jnp.sum(p, axis=1, keepdims=True)
        part = jnp.concatenate(
            [_dot(p[2 * hd:2 * hd + 2], dvp_refs[g][pl.ds(hd, page, stride=H_B), :]) for hd in range(H_B)], axis=0)
        pv = part if pv is None else pv + part
    md_ref[...] = m_new
    ld_ref[...] = l_new
    accd_ref[...] = alpha * accd_ref[...] + pv

    @pl.when(c == n_steps - 1)
    def _():
        o_lat = acca_ref[...] / la_ref[...]
        oa_ref[...] = jnp.concatenate(
            [_dot(o_lat[hd:hd + 1].astype(BF16), wuv_ref[hd]) for hd in range(H_A)], axis=1).astype(BF16)
        o = accd_ref[...] / ld_ref[...]
        lam = _lambda_value(lq1_ref, lk1_ref, lq2_ref, lk2_ref, lam_init)
        parts = []
        for hd in range(H_B):
            a = o[2 * hd:2 * hd + 1] - lam * o[2 * hd + 1:2 * hd + 2]
            parts.append(_rms(a, subln_ref[...]) * (1.0 - lam_init))
        ob_ref[...] = jnp.concatenate(parts, axis=1).astype(BF16)


def _decode_call(page_table, lat, krt, dkt, dvp, q, dq, k_self, dk, dv, relb8, wuv, lam_w, subln, lam_init):
    n_batch, n_pages = page_table.shape
    page = lat.shape[1]
    group = min(PAGES_PER_STEP, n_pages)
    assert n_pages % group == 0 and page >= MAX_DISTANCE
    n_maps = 2 * H_B
    bk_last = jnp.asarray(_bucket_np(page - np.arange(page))[None, :])

    def page_map(g):
        return lambda bi, c, pt: (pt[bi * n_pages + c * group + g], 0, 0)

    in_specs = []
    operands = []
    for g in range(group):
        in_specs += [
            pl.BlockSpec((None, page, D_C), page_map(g)),
            pl.BlockSpec((None, D_ROPE, page), page_map(g)),
            pl.BlockSpec((None, MIX_B, page), page_map(g)),
            pl.BlockSpec((None, H_B * page, 2 * D_B), page_map(g)),
        ]
        operands += [lat, krt, dkt, dvp]
    per_b = lambda bi, c, pt: (bi, 0, 0)
    const2 = lambda bi, c, pt: (0, 0)
    const3 = lambda bi, c, pt: (0, 0, 0)
    in_specs += [
        pl.BlockSpec((None, H_A, D_QK), per_b),
        pl.BlockSpec((None, 1, MIX_B), per_b),
        pl.BlockSpec((None, 1, D_QK), per_b),
        pl.BlockSpec((None, 1, MIX_B), per_b),
        pl.BlockSpec((None, 1, MIX_B), per_b),
        pl.BlockSpec((n_maps, N_BUCKETS), const2),
        pl.BlockSpec((1, page), const2),
        pl.BlockSpec(wuv.shape, const3),
        pl.BlockSpec((1, LANES), const2),
        pl.BlockSpec((1, LANES), const2),
        pl.BlockSpec((1, LANES), const2),
        pl.BlockSpec((1, LANES), const2),
        pl.BlockSpec((1, 2 * D_B), const2),
    ]
    operands += [q, dq, k_self, dk, dv, relb8, bk_last, wuv, *lam_w, subln]
    grid_spec = pltpu.PrefetchScalarGridSpec(
        num_scalar_prefetch=1,
        grid=(n_batch, n_pages // group),
        in_specs=in_specs,
        out_specs=[pl.BlockSpec((None, 1, MIX_A), per_b), pl.BlockSpec((None, 1, MIX_B), per_b)],
        scratch_shapes=[
            pltpu.VMEM((H_A, 1), F32), pltpu.VMEM((H_A, 1), F32), pltpu.VMEM((H_A, D_C), F32),
            pltpu.VMEM((n_maps, 1), F32), pltpu.VMEM((n_maps, 1), F32), pltpu.VMEM((n_maps, 2 * D_B), F32),
        ],
    )
    return pl.pallas_call(
        functools.partial(_decode_kernel, n_pages=n_pages, group=group, lam_init=lam_init),
        grid_spec=grid_spec,
        out_shape=[jax.ShapeDtypeStruct((n_batch, 1, MIX_A), BF16), jax.ShapeDtypeStruct((n_batch, 1, MIX_B), BF16)],
        compiler_params=pltpu.CompilerParams(dimension_semantics=("arbitrary", "arbitrary"),
                                             vmem_limit_bytes=VMEM_LIMIT),
        name="decode",
    )(page_table.reshape(-1), *operands)


def _post_kernel(x_ref, oa_ref, ob_ref, wo_ref, n2_ref, wup_ref, wdn_ref, fn_ref, y_ref, *, ff_chunk):
    attn = _dot(oa_ref[...], wo_ref[:MIX_A, :]) + _dot(ob_ref[...], wo_ref[MIX_A:, :])
    x1 = x_ref[...] + attn
    h2 = _rms(x1, n2_ref[...]).astype(BF16)
    d_ff = wup_ref.shape[1]
    mlp = None
    for j in range(d_ff // ff_chunk):
        u = _dot(h2, wup_ref[:, j * ff_chunk:(j + 1) * ff_chunk])
        g = jnp.square(jnp.maximum(u, 0.0)).astype(BF16)
        part = _dot(g, wdn_ref[j * ff_chunk:(j + 1) * ff_chunk, :])
        mlp = part if mlp is None else mlp + part
    y_ref[...] = _rms(x1 + mlp, fn_ref[...])


def _post_call(x2d, oa, ob, w):
    n, d_model = x2d.shape
    r = min(ROW_TILE, n)
    assert n % r == 0
    d_ff = w["w_up"].shape[1]
    row = lambda i: (i, 0)
    const = lambda i: (0, 0)
    return pl.pallas_call(
        functools.partial(_post_kernel, ff_chunk=min(1024, d_ff)),
        grid=(n // r,),
        in_specs=[
            pl.BlockSpec((r, d_model), row),
            pl.BlockSpec((r, MIX_A), row),
            pl.BlockSpec((r, MIX_B), row),
            pl.BlockSpec(w["w_o"].shape, const),
            pl.BlockSpec((1, d_model), const),
            pl.BlockSpec(w["w_up"].shape, const),
            pl.BlockSpec(w["w_down"].shape, const),
            pl.BlockSpec((1, d_model), const),
        ],
        out_specs=pl.BlockSpec((r, d_model), row),
        out_shape=jax.ShapeDtypeStruct((n, d_model), F32),
        compiler_params=pltpu.CompilerParams(dimension_semantics=("arbitrary",), vmem_limit_bytes=VMEM_LIMIT),
        name="post",
    )(x2d, oa, ob, w["w_o"], w["norm2"], w["w_up"], w["w_down"], w["final_norm"])


def _rope_tables(pos):
    half = D_ROPE // 2
    inv = ROPE_BASE ** (-jnp.arange(half, dtype=F32) / half)
    ang = pos.astype(F32)[:, None] * inv[None, :]
    cos, sin = jnp.cos(ang), jnp.sin(ang)
    ck = jnp.concatenate([cos, cos], axis=1)
    sk = jnp.concatenate([-sin, sin], axis=1)
    lane_pad = ((0, 0), (0, LANES - D_ROPE))
    return jnp.pad(ck, lane_pad), jnp.pad(sk, lane_pad), jnp.tile(ck, (1, H_A)), jnp.tile(sk, (1, H_A))


def _prep_weights(norm1, w_in, q_norm, w_uq, kv_norm, w_uk, w_uv, w_o, norm2, w_up, w_down, final_norm):
    d_model = w_in.shape[0]
    half = D_ROPE // 2
    cuts = np.cumsum([0, D_CQ, D_C, D_ROPE, MIX_B, MIX_B, MIX_B])
    c_q, c_kv, k_r, dq, dk, dv = (w_in[:, cuts[j]:cuts[j + 1]] for j in range(6))
    k_rr = jnp.concatenate([k_r[:, half:], k_r[:, :half]], axis=1)
    pad = jnp.zeros((d_model, LANES - D_ROPE), w_in.dtype)
    w_in_p = jnp.concatenate([c_q, c_kv, dq, dk, dv, k_r, pad, k_rr, pad], axis=1).astype(BF16)
    assert w_in_p.shape[1] == _IN_COLS_P
    wq = w_uq.reshape(D_CQ, H_A, D_NOPE + D_ROPE)
    nope = wq[:, :, :D_NOPE].reshape(D_CQ, H_A * D_NOPE)
    rope = wq[:, :, D_NOPE:]
    rope_rot = jnp.concatenate([rope[:, :, half:], rope[:, :, :half]], axis=2)
    w_uq_p = jnp.concatenate([nope, rope.reshape(D_CQ, -1), rope_rot.reshape(D_CQ, -1)], axis=1).astype(BF16)
    wk = jnp.transpose(w_uk, (1, 2, 0))
    zero = jnp.zeros((D_NOPE, D_C), w_uk.dtype)
    w_uk2 = jnp.stack([
        jnp.concatenate([jnp.concatenate([wk[2 * j], zero], axis=1), jnp.concatenate([zero, wk[2 * j + 1]], axis=1)], axis=0)
        for j in range(H_A // 2)]).astype(BF16)
    return {
        "norm1": norm1.reshape(1, -1), "w_in": w_in_p, "q_norm": q_norm.reshape(1, -1), "w_uq": w_uq_p,
        "kv_norm": kv_norm.reshape(1, -1), "w_uk2": w_uk2,
        "w_uv": jnp.transpose(w_uv, (1, 0, 2)).astype(BF16),
        "w_o": w_o.astype(BF16), "norm2": norm2.reshape(1, -1), "w_up": w_up.astype(BF16),
        "w_down": w_down.astype(BF16), "final_norm": final_norm.reshape(1, -1),
    }


def kernel(x_prompt, x_sample, cache_latent, cache_krope, cache_diff_k, cache_diff_v, page_table, meta_tokens, rel_bias,
           norm1, w_in, q_norm, w_uq, kv_norm, w_uk, w_uv, lambda_q1, lambda_k1, lambda_q2, lambda_k2, subln, w_o, norm2,
           w_up, w_down, final_norm):
    depth = norm1.shape[0]
    assert depth == 1, "single-layer trunk"
    layer = 0
    lam_init = 0.8 - 0.6 * math.exp(-0.3 * layer)
    b, seq, d_model = x_prompt.shape
    n_dec, dec_seq, _ = x_sample.shape
    assert dec_seq == 1
    n_phys, page = cache_latent.shape[1], cache_latent.shape[2]
    past_len = page_table.shape[1] * page

    w = _prep_weights(norm1[layer], w_in[layer], q_norm[layer], w_uq[layer], kv_norm[layer], w_uk[layer], w_uv[layer],
                      w_o[layer], norm2[layer], w_up[layer], w_down[layer], final_norm)
    lam_w = [jnp.pad(v[layer].reshape(1, -1), ((0, 0), (0, LANES - D_B)))
             for v in (lambda_q1, lambda_k1, lambda_q2, lambda_k2)]
    subln2 = subln[layer].reshape(1, -1)

    r_p = min(ROW_TILE, seq)
    tab_p = _rope_tables(N_META + jnp.arange(seq, dtype=jnp.int32))
    p_out = _proj_call(x_prompt.reshape(b * seq, d_model), tab_p, seq // r_p, w)
    tab_m = _rope_tables(jnp.arange(N_META, dtype=jnp.int32))
    m_out = _proj_call(meta_tokens.astype(x_prompt.dtype), tab_m, 1, w)
    tab_s = _rope_tables(jnp.full((n_dec,), past_len, dtype=jnp.int32))
    s_out = _proj_call(x_sample.reshape(n_dec, d_model), tab_s, 1, w)

    def keys(p_arr, m_arr):
        width = p_arr.shape[-1]
        m_pad = jnp.concatenate([m_arr, jnp.zeros((LANES - N_META, width), m_arr.dtype)], axis=0)
        return jnp.concatenate([p_arr.reshape(b, seq, width), jnp.broadcast_to(m_pad[None], (b, LANES, width))], axis=1)

    (p_ckv, p_kr, p_dk, p_dv, p_q, p_dqb, p_kb, p_dkb, p_dvb) = p_out
    (m_ckv, m_kr, m_dk, m_dv, _, _, m_kb, m_dkb, m_dvb) = m_out
    o_a = _mla_call(p_q, keys(p_kb, m_kb), w["w_uv"], seq)
    o_b = _diff_call(rel_bias, p_dqb.reshape(b, seq, MIX_B), keys(p_dkb, m_dkb), keys(p_dvb, m_dvb), lam_w, subln2,
                     seq, lam_init)
    y_prompt = _post_call(x_prompt.reshape(b * seq, d_model), o_a.reshape(b * seq, MIX_A), o_b.reshape(b * seq, MIX_B), w)

    (s_ckv, s_kr, s_dk, s_dv, s_q, s_dqb, _, _, _) = s_out
    k_self = jnp.concatenate([s_ckv, jnp.pad(s_kr, ((0, 0), (0, LANES - D_ROPE)))], axis=1)
    lat = cache_latent[layer]
    krt = jnp.transpose(cache_krope[layer], (0, 2, 1))
    dkt = jnp.transpose(cache_diff_k[layer], (0, 2, 3, 4, 1)).reshape(n_phys, MIX_B, page)
    dvp = cache_diff_v[layer].reshape(n_phys, page * H_B, 2 * D_B)
    relb8 = jnp.repeat(rel_bias.T, 2, axis=0)
    o_a_s, o_b_s = _decode_call(
        page_table, lat, krt, dkt, dvp,
        jnp.transpose(s_q, (1, 0, 2)).astype(F32), s_dqb.astype(F32).reshape(n_dec, 1, MIX_B),
        k_self.reshape(n_dec, 1, D_QK), s_dk.reshape(n_dec, 1, MIX_B), s_dv.reshape(n_dec, 1, MIX_B),
        relb8, w["w_uv"], lam_w, subln2, lam_init)
    y_sample = _post_call(x_sample.reshape(n_dec, d_model), o_a_s.reshape(n_dec, MIX_A), o_b_s.reshape(n_dec, MIX_B), w)

    def rows_p(m_arr, p_arr):
        width = p_arr.shape[-1]
        return jnp.concatenate([jnp.broadcast_to(m_arr[None], (b, N_META, width)), p_arr.reshape(b, seq, width)], axis=1)[None]

    t_all = N_META + seq
    return (
        y_prompt.reshape(b, seq, d_model),
        y_sample.reshape(n_dec, 1, d_model),
        rows_p(m_ckv, p_ckv),
        rows_p(m_kr, p_kr),
        rows_p(m_dk, p_dk).reshape(1, b, t_all, H_B, 2, D_B),
        rows_p(m_dv, p_dv).reshape(1, b, t_all, H_B, 2 * D_B),
        s_ckv.reshape(1, n_dec, 1, D_C),
        s_kr.reshape(1, n_dec, 1, D_ROPE),
        s_dk.reshape(1, n_dec, 1, H_B, 2, D_B),
        s_dv.reshape(1, n_dec, 1, H_B, 2 * D_B),
    )
```

```python
import functools
import math

import numpy as np
import jax
import jax.numpy as jnp
from jax import lax
from jax.experimental import pallas as pl
from jax.experimental.pallas import tpu as pltpu

N_META = 16
H_A = 8
D_NOPE = 64
D_ROPE = 32
D_VA = 64
D_C = 256
D_CQ = 384
H_B = 4
D_B = 64
MIX_A = H_A * D_VA
MIX_B = H_B * 2 * D_B
N_BUCKETS = 32
MAX_DISTANCE = 128
ROPE_BASE = 10000.0
EPS = 1e-6

LOG2E = math.log2(math.e)
SCALE_A = (D_NOPE + D_ROPE) ** -0.5 * LOG2E
SCALE_B = D_B ** -0.5 * LOG2E

LANES = 128
D_QK = D_C + LANES
ATT_TQ = 256
ATT_TK = 1024
DIFF_TQ = 512
DIFF_TK = 1024
ROW_TILE = 512
PAGES_PER_STEP = 32
VMEM_LIMIT = 56 * 1024 * 1024

_C_Q = 0
_C_KV = _C_Q + D_CQ
_C_DQ = _C_KV + D_C
_C_DK = _C_DQ + MIX_B
_C_DV = _C_DK + MIX_B
_C_KR = _C_DV + MIX_B
_C_KRR = _C_KR + LANES
_IN_COLS_P = _C_KRR + LANES

F32 = jnp.float32
BF16 = jnp.bfloat16
NEG_INF = float("-inf")


def _rms(x, g):
    return x * lax.rsqrt(jnp.mean(x * x, axis=-1, keepdims=True) + EPS) * g


def _dot(a, b):
    return jnp.dot(a, b, preferred_element_type=F32)


def _dot_nt(a, b):
    return lax.dot_general(a, b, (((1,), (1,)), ((), ())), preferred_element_type=F32)


def _bucket_np(dist):
    n = np.maximum(dist, 0)
    max_exact = N_BUCKETS // 2
    nf = np.maximum(n, max_exact).astype(np.float32)
    large = max_exact + (np.log(nf / np.float32(max_exact)) / np.float32(math.log(MAX_DISTANCE / max_exact))
                         * np.float32(N_BUCKETS - max_exact)).astype(np.int32)
    large = np.minimum(large, N_BUCKETS - 1)
    return np.where(n < max_exact, n, large).astype(np.int32)


def _proj_kernel(x_ref, n1_ref, win_ref, qn_ref, wuq_ref, kvn_ref, wuk_ref, ck_ref, sk_ref, cq_ref, sq_ref,
                 ckv_ref, kr_ref, dk_ref, dv_ref, q_ref, dqb_ref, kb_ref, dkb_ref, dvb_ref):
    h = _rms(x_ref[...], n1_ref[...]).astype(BF16)
    z = _dot(h, win_ref[...])
    c_kv = _rms(z[:, _C_KV:_C_KV + D_C], kvn_ref[...])
    ckv_ref[...] = c_kv
    k_rope = z[:, _C_KR:_C_KR + LANES] * ck_ref[...] + z[:, _C_KRR:_C_KRR + LANES] * sk_ref[...]
    kr_ref[...] = k_rope[:, :D_ROPE]
    kb_ref[...] = jnp.concatenate([c_kv, k_rope], axis=1).astype(BF16)
    dk = z[:, _C_DK:_C_DK + MIX_B]
    dk_ref[...] = dk
    dkb_ref[...] = dk.astype(BF16)
    dv = z[:, _C_DV:_C_DV + MIX_B]
    dv_ref[...] = dv
    dvb_ref[...] = dv.astype(BF16)
    dqb_ref[...] = (z[:, _C_DQ:_C_DQ + MIX_B] * SCALE_B).astype(BF16)
    c_q = _rms(z[:, _C_Q:_C_Q + D_CQ], qn_ref[...]).astype(BF16)
    qa = _dot(c_q, wuq_ref[...])
    n_nope = H_A * D_NOPE
    n_rope = H_A * D_ROPE
    q_rope = (qa[:, n_nope:n_nope + n_rope] * cq_ref[...] + qa[:, n_nope + n_rope:] * sq_ref[...]) * SCALE_A
    lane = lax.broadcasted_iota(jnp.int32, (q_rope.shape[0], LANES), 1)
    heads_per_tile = LANES // D_ROPE
    for pair in range(H_A // 2):
        qn2 = qa[:, pair * 2 * D_NOPE:(pair + 1) * 2 * D_NOPE].astype(BF16)
        ql2 = _dot(qn2, wuk_ref[pair]) * SCALE_A
        for j in range(2):
            hd = 2 * pair + j
            tile_r = q_rope[:, (hd // heads_per_tile) * LANES:(hd // heads_per_tile + 1) * LANES]
            shift = (LANES - (hd % heads_per_tile) * D_ROPE) % LANES
            if shift:
                tile_r = pltpu.roll(tile_r, shift, 1)
            qr128 = jnp.where(lane < D_ROPE, tile_r, 0.0)
            q_ref[hd] = jnp.concatenate([ql2[:, j * D_C:(j + 1) * D_C], qr128], axis=1).astype(BF16)


def _proj_call(x2d, tables, n_pos_tiles, w):
    n, d_model = x2d.shape
    r = min(ROW_TILE, n)
    assert n % r == 0
    steps = n // r
    ck, sk, cq, sq = tables
    row = lambda i: (i, 0)
    pos = lambda i: (i % n_pos_tiles, 0)
    const2 = lambda i: (0, 0)
    const3 = lambda i: (0, 0, 0)
    head = lambda i: (0, i, 0)
    in_specs = [
        pl.BlockSpec((r, d_model), row),
        pl.BlockSpec((1, d_model), const2),
        pl.BlockSpec(w["w_in"].shape, const2),
        pl.BlockSpec((1, D_CQ), const2),
        pl.BlockSpec(w["w_uq"].shape, const2),
        pl.BlockSpec((1, D_C), const2),
        pl.BlockSpec(w["w_uk2"].shape, const3),
        pl.BlockSpec((r, LANES), pos),
        pl.BlockSpec((r, LANES), pos),
        pl.BlockSpec((r, H_A * D_ROPE), pos),
        pl.BlockSpec((r, H_A * D_ROPE), pos),
    ]
    out_shape = [
        jax.ShapeDtypeStruct((n, D_C), F32),
        jax.ShapeDtypeStruct((n, D_ROPE), F32),
        jax.ShapeDtypeStruct((n, MIX_B), F32),
        jax.ShapeDtypeStruct((n, MIX_B), F32),
        jax.ShapeDtypeStruct((H_A, n, D_QK), BF16),
        jax.ShapeDtypeStruct((n, MIX_B), BF16),
        jax.ShapeDtypeStruct((n, D_QK), BF16),
        jax.ShapeDtypeStruct((n, MIX_B), BF16),
        jax.ShapeDtypeStruct((n, MIX_B), BF16),
    ]
    out_specs = [
        pl.BlockSpec((r, D_C), row),
        pl.BlockSpec((r, D_ROPE), row),
        pl.BlockSpec((r, MIX_B), row),
        pl.BlockSpec((r, MIX_B), row),
        pl.BlockSpec((H_A, r, D_QK), head),
        pl.BlockSpec((r, MIX_B), row),
        pl.BlockSpec((r, D_QK), row),
        pl.BlockSpec((r, MIX_B), row),
        pl.BlockSpec((r, MIX_B), row),
    ]
    return pl.pallas_call(
        _proj_kernel,
        grid=(steps,),
        in_specs=in_specs,
        out_specs=out_specs,
        out_shape=out_shape,
        compiler_params=pltpu.CompilerParams(dimension_semantics=("arbitrary",), vmem_limit_bytes=VMEM_LIMIT),
        name="proj",
    )(x2d, w["norm1"], w["w_in"], w["q_norm"], w["w_uq"], w["kv_norm"], w["w_uk2"], ck, sk, cq, sq)


def _rep(x, n):
    return jnp.concatenate([x] * n, axis=1) if n > 1 else x


def _softmax_init(s, v, m_ref, l_ref, acc_ref):
    n = s.shape[1] // LANES
    m = jnp.broadcast_to(jnp.max(s, axis=1, keepdims=True), (s.shape[0], LANES))
    p = jnp.exp2(s - _rep(m, n))
    m_ref[...] = m
    psum = p[:, :LANES]
    for j in range(1, n):
        psum = psum + p[:, j * LANES:(j + 1) * LANES]
    l_ref[...] = psum
    acc_ref[...] = _dot(p.astype(BF16), v)


def _softmax_step(s, v, m_ref, l_ref, acc_ref):
    n = s.shape[1] // LANES
    m_prev = m_ref[...]
    m_new = jnp.maximum(m_prev, jnp.max(s, axis=1, keepdims=True))
    alpha = jnp.exp2(m_prev - m_new)
    p = jnp.exp2(s - _rep(m_new, n))
    psum = p[:, :LANES]
    for j in range(1, n):
        psum = psum + p[:, j * LANES:(j + 1) * LANES]
    l_ref[...] = alpha * l_ref[...] + psum
    acc_ref[...] = _rep(alpha, acc_ref.shape[1] // LANES) * acc_ref[...] + _dot(p.astype(BF16), v)
    m_ref[...] = m_new


def _mla_kernel(q_ref, k_ref, wuv_ref, o_ref, m_ref, l_ref, acc_ref, *, seq, tile, tk):
    i = pl.program_id(1)
    rows = H_A * tile
    q = q_ref[...].reshape(rows, D_QK)

    def scores(start, size):
        k = k_ref[pl.ds(start, size), :]
        return _dot_nt(q, k), k[:, :D_C]

    s, kl = scores(seq, LANES)
    col = lax.broadcasted_iota(jnp.int32, (rows, LANES), 1)
    _softmax_init(jnp.where(col < N_META, s, NEG_INF), kl, m_ref, l_ref, acc_ref)

    n_far = i * tile
    n_big = n_far // tk

    def big(kt, carry):
        s, kl = scores(pl.multiple_of(kt * tk, tk), tk)
        _softmax_step(s, kl, m_ref, l_ref, acc_ref)
        return carry

    lax.fori_loop(0, n_big, big, 0)

    def small(j, carry):
        s, kl = scores(pl.multiple_of(n_big * tk + j * tile, tile), tile)
        _softmax_step(s, kl, m_ref, l_ref, acc_ref)
        return carry

    lax.fori_loop(0, (n_far - n_big * tk) // tile, small, 0)

    s, kl = scores(pl.multiple_of(i * tile, tile), tile)
    r_in = lax.broadcasted_iota(jnp.int32, (H_A, tile, tile), 1).reshape(rows, tile)
    c_in = lax.broadcasted_iota(jnp.int32, (rows, tile), 1)
    _softmax_step(jnp.where(c_in <= r_in, s, NEG_INF), kl, m_ref, l_ref, acc_ref)

    o_lat = (acc_ref[...] / jnp.sum(l_ref[...], axis=1, keepdims=True)).astype(BF16)
    o_ref[...] = jnp.concatenate(
        [_dot(o_lat[hd * tile:(hd + 1) * tile], wuv_ref[hd]) for hd in range(H_A)], axis=1).astype(BF16)


def _mla_call(q, k, wuv, seq):
    b = k.shape[0]
    tile = ATT_TQ
    tk = ATT_TK
    assert seq % tile == 0 and tile >= MAX_DISTANCE and tk % tile == 0
    nk = k.shape[1]
    rows = H_A * tile
    nq = seq // tile
    return pl.pallas_call(
        functools.partial(_mla_kernel, seq=seq, tile=tile, tk=tk),
        grid=(b, nq),
        in_specs=[
            pl.BlockSpec((H_A, tile, D_QK), lambda bi, i: (0, bi * nq + i, 0)),
            pl.BlockSpec((None, nk, D_QK), lambda bi, i: (bi, 0, 0)),
            pl.BlockSpec(wuv.shape, lambda bi, i: (0, 0, 0)),
        ],
        out_specs=pl.BlockSpec((None, tile, MIX_A), lambda bi, i: (bi, i, 0)),
        out_shape=jax.ShapeDtypeStruct((b, seq, MIX_A), BF16),
        scratch_shapes=[pltpu.VMEM((rows, LANES), F32), pltpu.VMEM((rows, LANES), F32), pltpu.VMEM((rows, D_C), F32)],
        compiler_params=pltpu.CompilerParams(dimension_semantics=("arbitrary", "arbitrary"),
                                             vmem_limit_bytes=VMEM_LIMIT),
        name="mla",
    )(q, k, wuv)


def _lambda_value(lq1_ref, lk1_ref, lq2_ref, lk2_ref, lam_init):
    s1 = jnp.sum(lq1_ref[...] * lk1_ref[...], axis=1, keepdims=True)
    s2 = jnp.sum(lq2_ref[...] * lk2_ref[...], axis=1, keepdims=True)
    return jnp.exp(s1) - jnp.exp(s2) + lam_init


def _bias_from_buckets(bk, relb_ref, hd):
    far = relb_ref[N_BUCKETS - 1, hd]
    bias = jnp.zeros(bk.shape, F32)
    for bkt in range(N_BUCKETS - 1):
        bias = jnp.where(bk == bkt, (relb_ref[bkt, hd] - far) * LOG2E, bias)
    return bias


def _diff_kernel(relb_ref, q_ref, k_ref, v_ref, bkd_ref, bkp_ref, bkm_ref, lq1_ref, lk1_ref, lq2_ref, lk2_ref,
                 subln_ref, o_ref, bd_ref, bp_ref, bm_ref, m_ref, l_ref, acc_ref, *, seq, tile, tk, lam_init):
    hd = pl.program_id(1)
    i = pl.program_id(2)
    width = 2 * D_B

    @pl.when(i == 0)
    def _():
        bd = _bias_from_buckets(bkd_ref[...], relb_ref, hd)
        bd_ref[...] = jnp.concatenate([bd, bd], axis=0)
        bp = _bias_from_buckets(bkp_ref[...], relb_ref, hd)
        bp_ref[...] = jnp.concatenate([bp, bp], axis=0)
        bm = _bias_from_buckets(bkm_ref[...], relb_ref, hd)
        bm_ref[...] = jnp.concatenate([bm, bm], axis=0)

    q = q_ref[...]
    lane = lax.broadcasted_iota(jnp.int32, (tile, width), 1)
    zero = jnp.zeros_like(q)
    qbd = jnp.concatenate([jnp.where(lane < D_B, q, zero), jnp.where(lane >= D_B, q, zero)], axis=0)

    def scores(start, size):
        return _dot_nt(qbd, k_ref[pl.ds(start, size), :]), v_ref[pl.ds(start, size), :]

    s, v = scores(seq, LANES)
    s = s + jnp.where(i == 0, 1.0, 0.0) * bm_ref[...]
    col = lax.broadcasted_iota(jnp.int32, (2 * tile, LANES), 1)
    _softmax_init(jnp.where(col < N_META, s, NEG_INF), v, m_ref, l_ref, acc_ref)

    n_far = jnp.maximum(i - 1, 0) * tile
    n_big = n_far // tk

    def big(kt, carry):
        s, v = scores(pl.multiple_of(kt * tk, tk), tk)
        _softmax_step(s, v, m_ref, l_ref, acc_ref)
        return carry

    lax.fori_loop(0, n_big, big, 0)

    def small(j, carry):
        s, v = scores(pl.multiple_of(n_big * tk + j * tile, tile), tile)
        _softmax_step(s, v, m_ref, l_ref, acc_ref)
        return carry

    lax.fori_loop(0, (n_far - n_big * tk) // tile, small, 0)

    @pl.when(i > 0)
    def _():
        s, v = scores(pl.multiple_of((i - 1) * tile, tile), tile)
        _softmax_step(s + bp_ref[...], v, m_ref, l_ref, acc_ref)

    s, v = scores(pl.multiple_of(i * tile, tile), tile)
    r_in = lax.broadcasted_iota(jnp.int32, (2, tile, tile), 1).reshape(2 * tile, tile)
    c_in = lax.broadcasted_iota(jnp.int32, (2 * tile, tile), 1)
    _softmax_step(jnp.where(c_in <= r_in, s + bd_ref[...], NEG_INF), v, m_ref, l_ref, acc_ref)

    o = acc_ref[...] / jnp.sum(l_ref[...], axis=1, keepdims=True)
    lam = _lambda_value(lq1_ref, lk1_ref, lq2_ref, lk2_ref, lam_init)
    a = o[:tile] - lam * o[tile:]
    o_ref[...] = (_rms(a, subln_ref[...]) * (1.0 - lam_init)).astype(BF16)


def _diff_call(relb, dq, dk, dv, lam_w, subln, seq, lam_init):
    b = dq.shape[0]
    tile = DIFF_TQ
    tk = DIFF_TK
    assert seq % tile == 0 and tile >= MAX_DISTANCE and tk % tile == 0
    nk = dk.shape[1]
    width = 2 * D_B
    r = np.arange(tile)[:, None]
    c = np.arange(tile)[None, :]
    bk_diag = jnp.asarray(_bucket_np(r - c))
    bk_prev = jnp.asarray(_bucket_np(tile + r - c))
    bk_meta = jnp.asarray(_bucket_np(N_META + r - np.arange(LANES)[None, :]))
    const = lambda bi, h, i: (0, 0)
    smem = pl.BlockSpec(memory_space=pltpu.SMEM)
    return pl.pallas_call(
        functools.partial(_diff_kernel, seq=seq, tile=tile, tk=tk, lam_init=lam_init),
        grid=(b, H_B, seq // tile),
        in_specs=[
            smem,
            pl.BlockSpec((None, tile, width), lambda bi, h, i: (bi, i, h)),
            pl.BlockSpec((None, nk, width), lambda bi, h, i: (bi, 0, h)),
            pl.BlockSpec((None, nk, width), lambda bi, h, i: (bi, 0, h)),
            pl.BlockSpec((tile, tile), const),
            pl.BlockSpec((tile, tile), const),
            pl.BlockSpec((tile, LANES), const),
            pl.BlockSpec((1, LANES), const),
            pl.BlockSpec((1, LANES), const),
            pl.BlockSpec((1, LANES), const),
            pl.BlockSpec((1, LANES), const),
            pl.BlockSpec((1, width), const),
        ],
        out_specs=pl.BlockSpec((None, tile, width), lambda bi, h, i: (bi, i, h)),
        out_shape=jax.ShapeDtypeStruct((b, seq, MIX_B), BF16),
        scratch_shapes=[
            pltpu.VMEM((2 * tile, tile), F32),
            pltpu.VMEM((2 * tile, tile), F32),
            pltpu.VMEM((2 * tile, LANES), F32),
            pltpu.VMEM((2 * tile, LANES), F32),
            pltpu.VMEM((2 * tile, LANES), F32),
            pltpu.VMEM((2 * tile, width), F32),
        ],
        compiler_params=pltpu.CompilerParams(dimension_semantics=("arbitrary", "arbitrary", "arbitrary"),
                                             vmem_limit_bytes=VMEM_LIMIT),
        name="diff",
    )(relb, dq, dk, dv, bk_diag, bk_prev, bk_meta, *lam_w, subln)


def _decode_kernel(pt_ref, *refs, n_pages, group, lam_init):
    del pt_ref
    page_refs = refs[:4 * group]
    (q_ref, dq_ref, kself_ref, dk_ref, dv_ref, relb_ref, bkl_ref, wuv_ref,
     lq1_ref, lk1_ref, lq2_ref, lk2_ref, subln_ref,
     oa_ref, ob_ref, ma_ref, la_ref, acca_ref, md_ref, ld_ref, accd_ref) = refs[4 * group:]
    lat_refs = page_refs[0::4]
    krt_refs = page_refs[1::4]
    dkt_refs = page_refs[2::4]
    dvp_refs = page_refs[3::4]
    c = pl.program_id(1)
    n_steps = n_pages // group
    n_maps = 2 * H_B

    q = q_ref[...]
    ql = q[:, :D_C]
    qr = q[:, D_C:D_C + D_ROPE]
    row = lax.broadcasted_iota(jnp.int32, (n_maps, MIX_B), 0)
    lane_blk = lax.broadcasted_iota(jnp.int32, (n_maps, MIX_B), 1) // D_B
    dqbd = jnp.where(row == lane_blk, jnp.broadcast_to(dq_ref[...], (n_maps, MIX_B)), 0.0)
    relb = relb_ref[...]
    far = relb[:, N_BUCKETS - 1:N_BUCKETS]

    @pl.when(c == 0)
    def _():
        k_self = kself_ref[...]
        ckv = k_self[:, :D_C]
        lane0 = jnp.where(lax.broadcasted_iota(jnp.int32, (H_A, LANES), 1) == 0, 1.0, 0.0)
        ma_ref[...] = jnp.broadcast_to(jnp.sum(q * k_self, axis=1, keepdims=True), (H_A, LANES))
        la_ref[...] = lane0
        acca_ref[...] = jnp.broadcast_to(ckv, (H_A, D_C))
        md_ref[...] = jnp.broadcast_to(
            jnp.sum(dqbd * dk_ref[...], axis=1, keepdims=True) + (relb[:, 0:1] - far) * LOG2E, (n_maps, LANES))
        ld_ref[...] = lane0
        dv = dv_ref[...]
        accd_ref[...] = jnp.concatenate(
            [jnp.broadcast_to(dv[:, (r // 2) * 2 * D_B:(r // 2 + 1) * 2 * D_B], (1, 2 * D_B)) for r in range(n_maps)], axis=0)

    s_a = [_dot_nt(ql, lat_refs[g][...]) + _dot(qr, krt_refs[g][...]) for g in range(group)]
    m_prev = ma_ref[...]
    m_lane = s_a[0]
    for s in s_a[1:]:
        m_lane = jnp.maximum(m_lane, s)
    m_new = jnp.maximum(m_prev, jnp.max(m_lane, axis=1, keepdims=True))
    alpha = jnp.exp2(m_prev - m_new)
    l_new = alpha * la_ref[...]
    pv = None
    for g in range(group):
        p = jnp.exp2(s_a[g] - m_new)
        l_new = l_new + p
        part = _dot(p, lat_refs[g][...])
        pv = part if pv is None else pv + part
    ma_ref[...] = m_new
    la_ref[...] = l_new
    acca_ref[...] = _rep(alpha, D_C // LANES) * acca_ref[...] + pv

    bkl = bkl_ref[...]
    bias_last = jnp.zeros((n_maps, bkl.shape[1]), F32)
    for bkt in range(N_BUCKETS - 1):
        bias_last = jnp.where(bkl == bkt, (relb[:, bkt:bkt + 1] - far) * LOG2E, bias_last)
    s_d = [_dot(dqbd, dkt_refs[g][...]) for g in range(group)]
    s_d[group - 1] = s_d[group - 1] + jnp.where(c == n_steps - 1, 1.0, 0.0) * bias_last
    m_prev = md_ref[...]
    m_lane = s_d[0]
    for s in s_d[1:]:
        m_lane = jnp.maximum(m_lane, s)
    m_new = jnp.maximum(m_prev, jnp.max(m_lane, axis=1, keepdims=True))
    alpha = jnp.exp2(m_prev - m_new)
    l_new = alpha * ld_ref[...]
    page = bkl.shape[1]
    pv = None
    for g in range(group):
        p = jnp.exp2(s_d[g] - m_new)
        l_new = l_new + p
        part = jnp.concatenate(
            [_dot(p[2 * hd:2 * hd + 2], dvp_refs[g][pl.ds(hd, page, stride=H_B), :]) for hd in range(H_B)], axis=0)
        pv = part if pv is None else pv + part
    md_ref[...] = m_new
    ld_ref[...] = l_new
    accd_ref[...] = alpha * accd_ref[...] + pv

    @pl.when(c == n_steps - 1)
    def _():
        o_lat = acca_ref[...] / jnp.sum(la_ref[...], axis=1, keepdims=True)
        oa_ref[...] = jnp.concatenate(
            [_dot(o_lat[hd:hd + 1].astype(BF16), wuv_ref[hd]) for hd in range(H_A)], axis=1).astype(BF16)
        o = accd_ref[...] / jnp.sum(ld_ref[...], axis=1, keepdims=True)
        lam = _lambda_value(lq1_ref, lk1_ref, lq2_ref, lk2_ref, lam_init)
        parts = []
        for hd in range(H_B):
            a = o[2 * hd:2 * hd + 1] - lam * o[2 * hd + 1:2 * hd + 2]
            parts.append(_rms(a, subln_ref[...]) * (1.0 - lam_init))
        ob_ref[...] = jnp.concatenate(parts, axis=1).astype(BF16)


def _decode_call(page_table, lat, krt, dkt, dvp, q, dq, k_self, dk, dv, relb8, wuv, lam_w, subln, lam_init):
    n_batch, n_pages = page_table.shape
    page = lat.shape[1]
    group = min(PAGES_PER_STEP, n_pages)
    assert n_pages % group == 0 and page >= MAX_DISTANCE
    n_maps = 2 * H_B
    bk_last = jnp.asarray(_bucket_np(page - np.arange(page))[None, :])

    def page_map(g):
        return lambda bi, c, pt: (pt[bi * n_pages + c * group + g], 0, 0)

    in_specs = []
    operands = []
    for g in range(group):
        in_specs += [
            pl.BlockSpec((None, page, D_C), page_map(g)),
            pl.BlockSpec((None, D_ROPE, page), page_map(g)),
            pl.BlockSpec((None, MIX_B, page), page_map(g)),
            pl.BlockSpec((None, H_B * page, 2 * D_B), page_map(g)),
        ]
        operands += [lat, krt, dkt, dvp]
    per_b = lambda bi, c, pt: (bi, 0, 0)
    const2 = lambda bi, c, pt: (0, 0)
    const3 = lambda bi, c, pt: (0, 0, 0)
    in_specs += [
        pl.BlockSpec((None, H_A, D_QK), per_b),
        pl.BlockSpec((None, 1, MIX_B), per_b),
        pl.BlockSpec((None, 1, D_QK), per_b),
        pl.BlockSpec((None, 1, MIX_B), per_b),
        pl.BlockSpec((None, 1, MIX_B), per_b),
        pl.BlockSpec((n_maps, N_BUCKETS), const2),
        pl.BlockSpec((1, page), const2),
        pl.BlockSpec(wuv.shape, const3),
        pl.BlockSpec((1, LANES), const2),
        pl.BlockSpec((1, LANES), const2),
        pl.BlockSpec((1, LANES), const2),
        pl.BlockSpec((1, LANES), const2),
        pl.BlockSpec((1, 2 * D_B), const2),
    ]
    operands += [q, dq, k_self, dk, dv, relb8, bk_last, wuv, *lam_w, subln]
    grid_spec = pltpu.PrefetchScalarGridSpec(
        num_scalar_prefetch=1,
        grid=(n_batch, n_pages // group),
        in_specs=in_specs,
        out_specs=[pl.BlockSpec((None, 1, MIX_A), per_b), pl.BlockSpec((None, 1, MIX_B), per_b)],
        scratch_shapes=[
            pltpu.VMEM((H_A, LANES), F32), pltpu.VMEM((H_A, LANES), F32), pltpu.VMEM((H_A, D_C), F32),
            pltpu.VMEM((n_maps, LANES), F32), pltpu.VMEM((n_maps, LANES), F32), pltpu.VMEM((n_maps, 2 * D_B), F32),
        ],
    )
    return pl.pallas_call(
        functools.partial(_decode_kernel, n_pages=n_pages, group=group, lam_init=lam_init),
        grid_spec=grid_spec,
        out_shape=[jax.ShapeDtypeStruct((n_batch, 1, MIX_A), BF16), jax.ShapeDtypeStruct((n_batch, 1, MIX_B), BF16)],
        compiler_params=pltpu.CompilerParams(dimension_semantics=("arbitrary", "arbitrary"),
                                             vmem_limit_bytes=VMEM_LIMIT),
        name="decode",
    )(page_table.reshape(-1), *operands)


def _post_kernel(x_ref, oa_ref, ob_ref, wo_ref, n2_ref, wup_ref, wdn_ref, fn_ref, y_ref, *, ff_chunk):
    attn = _dot(oa_ref[...], wo_ref[:MIX_A, :]) + _dot(ob_ref[...], wo_ref[MIX_A:, :])
    x1 = x_ref[...] + attn
    h2 = _rms(x1, n2_ref[...]).astype(BF16)
    d_ff = wup_ref.shape[1]
    mlp = None
    for j in range(d_ff // ff_chunk):
        u = _dot(h2, wup_ref[:, j * ff_chunk:(j + 1) * ff_chunk])
        g = jnp.square(jnp.maximum(u, 0.0)).astype(BF16)
        part = _dot(g, wdn_ref[j * ff_chunk:(j + 1) * ff_chunk, :])
        mlp = part if mlp is None else mlp + part
    y_ref[...] = _rms(x1 + mlp, fn_ref[...])


def _post_call(x2d, oa, ob, w):
    n, d_model = x2d.shape
    r = min(ROW_TILE, n)
    assert n % r == 0
    d_ff = w["w_up"].shape[1]
    row = lambda i: (i, 0)
    const = lambda i: (0, 0)
    return pl.pallas_call(
        functools.partial(_post_kernel, ff_chunk=min(1024, d_ff)),
        grid=(n // r,),
        in_specs=[
            pl.BlockSpec((r, d_model), row),
            pl.BlockSpec((r, MIX_A), row),
            pl.BlockSpec((r, MIX_B), row),
            pl.BlockSpec(w["w_o"].shape, const),
            pl.BlockSpec((1, d_model), const),
            pl.BlockSpec(w["w_up"].shape, const),
            pl.BlockSpec(w["w_down"].shape, const),
            pl.BlockSpec((1, d_model), const),
        ],
        out_specs=pl.BlockSpec((r, d_model), row),
        out_shape=jax.ShapeDtypeStruct((n, d_model), F32),
        compiler_params=pltpu.CompilerParams(dimension_semantics=("arbitrary",), vmem_limit_bytes=VMEM_LIMIT),
        name="post",
    )(x2d, oa, ob, w["w_o"], w["norm2"], w["w_up"], w["w_down"], w["final_norm"])


def _rope_tables(pos):
    half = D_ROPE // 2
    inv = ROPE_BASE ** (-jnp.arange(half, dtype=F32) / half)
    ang = pos.astype(F32)[:, None] * inv[None, :]
    cos, sin = jnp.cos(ang), jnp.sin(ang)
    ck = jnp.concatenate([cos, cos], axis=1)
    sk = jnp.concatenate([-sin, sin], axis=1)
    lane_pad = ((0, 0), (0, LANES - D_ROPE))
    return jnp.pad(ck, lane_pad), jnp.pad(sk, lane_pad), jnp.tile(ck, (1, H_A)), jnp.tile(sk, (1, H_A))


def _prep_weights(norm1, w_in, q_norm, w_uq, kv_norm, w_uk, w_uv, w_o, norm2, w_up, w_down, final_norm):
    d_model = w_in.shape[0]
    half = D_ROPE // 2
    cuts = np.cumsum([0, D_CQ, D_C, D_ROPE, MIX_B, MIX_B, MIX_B])
    c_q, c_kv, k_r, dq, dk, dv = (w_in[:, cuts[j]:cuts[j + 1]] for j in range(6))
    k_rr = jnp.concatenate([k_r[:, half:], k_r[:, :half]], axis=1)
    pad = jnp.zeros((d_model, LANES - D_ROPE), w_in.dtype)
    w_in_p = jnp.concatenate([c_q, c_kv, dq, dk, dv, k_r, pad, k_rr, pad], axis=1).astype(BF16)
    assert w_in_p.shape[1] == _IN_COLS_P
    wq = w_uq.reshape(D_CQ, H_A, D_NOPE + D_ROPE)
    nope = wq[:, :, :D_NOPE].reshape(D_CQ, H_A * D_NOPE)
    rope = wq[:, :, D_NOPE:]
    rope_rot = jnp.concatenate([rope[:, :, half:], rope[:, :, :half]], axis=2)
    w_uq_p = jnp.concatenate([nope, rope.reshape(D_CQ, -1), rope_rot.reshape(D_CQ, -1)], axis=1).astype(BF16)
    wk = jnp.transpose(w_uk, (1, 2, 0))
    zero = jnp.zeros((D_NOPE, D_C), w_uk.dtype)
    w_uk2 = jnp.stack([
        jnp.concatenate([jnp.concatenate([wk[2 * j], zero], axis=1), jnp.concatenate([zero, wk[2 * j + 1]], axis=1)], axis=0)
        for j in range(H_A // 2)]).astype(BF16)
    return {
        "norm1": norm1.reshape(1, -1), "w_in": w_in_p, "q_norm": q_norm.reshape(1, -1), "w_uq": w_uq_p,
        "kv_norm": kv_norm.reshape(1, -1), "w_uk2": w_uk2,
        "w_uv": jnp.transpose(w_uv, (1, 0, 2)).astype(BF16),
        "w_o": w_o.astype(BF16), "norm2": norm2.reshape(1, -1), "w_up": w_up.astype(BF16),
        "w_down": w_down.astype(BF16), "final_norm": final_norm.reshape(1, -1),
    }


def kernel(x_prompt, x_sample, cache_latent, cache_krope, cache_diff_k, cache_diff_v, page_table, meta_tokens, rel_bias,
           norm1, w_in, q_norm, w_uq, kv_norm, w_uk, w_uv, lambda_q1, lambda_k1, lambda_q2, lambda_k2, subln, w_o, norm2,
           w_up, w_down, final_norm):
    depth = norm1.shape[0]
    assert depth == 1, "single-layer trunk"
    layer = 0
    lam_init = 0.8 - 0.6 * math.exp(-0.3 * layer)
    b, seq, d_model = x_prompt.shape
    n_dec, dec_seq, _ = x_sample.shape
    assert dec_seq == 1
    n_phys, page = cache_latent.shape[1], cache_latent.shape[2]
    past_len = page_table.shape[1] * page

    w = _prep_weights(norm1[layer], w_in[layer], q_norm[layer], w_uq[layer], kv_norm[layer], w_uk[layer], w_uv[layer],
                      w_o[layer], norm2[layer], w_up[layer], w_down[layer], final_norm)
    lam_w = [jnp.pad(v[layer].reshape(1, -1), ((0, 0), (0, LANES - D_B)))
             for v in (lambda_q1, lambda_k1, lambda_q2, lambda_k2)]
    subln2 = subln[layer].reshape(1, -1)

    r_p = min(ROW_TILE, seq)
    tab_p = _rope_tables(N_META + jnp.arange(seq, dtype=jnp.int32))
    p_out = _proj_call(x_prompt.reshape(b * seq, d_model), tab_p, seq // r_p, w)
    tab_m = _rope_tables(jnp.arange(N_META, dtype=jnp.int32))
    m_out = _proj_call(meta_tokens.astype(x_prompt.dtype), tab_m, 1, w)
    tab_s = _rope_tables(jnp.full((n_dec,), past_len, dtype=jnp.int32))
    s_out = _proj_call(x_sample.reshape(n_dec, d_model), tab_s, 1, w)

    def keys(p_arr, m_arr):
        width = p_arr.shape[-1]
        m_pad = jnp.concatenate([m_arr, jnp.zeros((LANES - N_META, width), m_arr.dtype)], axis=0)
        return jnp.concatenate([p_arr.reshape(b, seq, width), jnp.broadcast_to(m_pad[None], (b, LANES, width))], axis=1)

    (p_ckv, p_kr, p_dk, p_dv, p_q, p_dqb, p_kb, p_dkb, p_dvb) = p_out
    (m_ckv, m_kr, m_dk, m_dv, _, _, m_kb, m_dkb, m_dvb) = m_out
    o_a = _mla_call(p_q, keys(p_kb, m_kb), w["w_uv"], seq)
    o_b = _diff_call(rel_bias, p_dqb.reshape(b, seq, MIX_B), keys(p_dkb, m_dkb), keys(p_dvb, m_dvb), lam_w, subln2,
                     seq, lam_init)
    y_prompt = _post_call(x_prompt.reshape(b * seq, d_model), o_a.reshape(b * seq, MIX_A), o_b.reshape(b * seq, MIX_B), w)

    (s_ckv, s_kr, s_dk, s_dv, s_q, s_dqb, _, _, _) = s_out
    k_self = jnp.concatenate([s_ckv, jnp.pad(s_kr, ((0, 0), (0, LANES - D_ROPE)))], axis=1)
    lat = cache_latent[layer]
    krt = jnp.transpose(cache_krope[layer], (0, 2, 1))
    dkt = jnp.transpose(cache_diff_k[layer], (0, 2, 3, 4, 1)).reshape(n_phys, MIX_B, page)
    dvp = cache_diff_v[layer].reshape(n_phys, page * H_B, 2 * D_B)
    relb8 = jnp.repeat(rel_bias.T, 2, axis=0)
    o_a_s, o_b_s = _decode_call(
        page_table, lat, krt, dkt, dvp,
        jnp.transpose(s_q, (1, 0, 2)).astype(F32), s_dqb.astype(F32).reshape(n_dec, 1, MIX_B),
        k_self.reshape(n_dec, 1, D_QK), s_dk.reshape(n_dec, 1, MIX_B), s_dv.reshape(n_dec, 1, MIX_B),
        relb8, w["w_uv"], lam_w, subln2, lam_init)
    y_sample = _post_call(x_sample.reshape(n_dec, d_model), o_a_s.reshape(n_dec, MIX_A), o_b_s.reshape(n_dec, MIX_B), w)

    def rows_p(m_arr, p_arr):
        width = p_arr.shape[-1]
        return jnp.concatenate([jnp.broadcast_to(m_arr[None], (b, N_META, width)), p_arr.reshape(b, seq, width)], axis=1)[None]

    t_all = N_META + seq
    return (
        y_prompt.reshape(b, seq, d_model),
        y_sample.reshape(n_dec, 1, d_model),
        rows_p(m_ckv, p_ckv),
        rows_p(m_kr, p_kr),
        rows_p(m_dk, p_dk).reshape(1, b, t_all, H_B, 2, D_B),
        rows_p(m_dv, p_dv).reshape(1, b, t_all, H_B, 2 * D_B),
        s_ckv.reshape(1, n_dec, 1, D_C),
        s_kr.reshape(1, n_dec, 1, D_ROPE),
        s_dk.reshape(1, n_dec, 1, H_B, 2, D_B),
        s_dv.reshape(1, n_dec, 1, H_B, 2 * D_B),
    )
```

```python
import functools
import math

import numpy as np
import jax
import jax.numpy as jnp
from jax import lax
from jax.experimental import pallas as pl
from jax.experimental.pallas import tpu as pltpu

N_META = 16
H_A = 8
D_NOPE = 64
D_ROPE = 32
D_VA = 64
D_C = 256
D_CQ = 384
H_B = 4
D_B = 64
MIX_A = H_A * D_VA
MIX_B = H_B * 2 * D_B
N_BUCKETS = 32
MAX_DISTANCE = 128
ROPE_BASE = 10000.0
EPS = 1e-6

LOG2E = math.log2(math.e)
SCALE_A = (D_NOPE + D_ROPE) ** -0.5 * LOG2E
SCALE_B = D_B ** -0.5 * LOG2E

LANES = 128
D_QK = D_C + LANES
ATT_TQ = 256
ATT_TK = 1024
DIFF_TQ = 512
DIFF_TK = 1024
ROW_TILE = 512
PAGES_PER_STEP = 32
VMEM_LIMIT = 56 * 1024 * 1024

_C_Q = 0
_C_KV = _C_Q + D_CQ
_C_DQ = _C_KV + D_C
_C_DK = _C_DQ + MIX_B
_C_DV = _C_DK + MIX_B
_C_KR = _C_DV + MIX_B
_C_KRR = _C_KR + LANES
_IN_COLS_P = _C_KRR + LANES

F32 = jnp.float32
BF16 = jnp.bfloat16
NEG_INF = float("-inf")


def _rms(x, g):
    return x * lax.rsqrt(jnp.mean(x * x, axis=-1, keepdims=True) + EPS) * g


def _dot(a, b):
    return jnp.dot(a, b, preferred_element_type=F32)


def _dot_nt(a, b):
    return lax.dot_general(a, b, (((1,), (1,)), ((), ())), preferred_element_type=F32)


def _bucket_np(dist):
    n = np.maximum(dist, 0)
    max_exact = N_BUCKETS // 2
    nf = np.maximum(n, max_exact).astype(np.float32)
    large = max_exact + (np.log(nf / np.float32(max_exact)) / np.float32(math.log(MAX_DISTANCE / max_exact))
                         * np.float32(N_BUCKETS - max_exact)).astype(np.int32)
    large = np.minimum(large, N_BUCKETS - 1)
    return np.where(n < max_exact, n, large).astype(np.int32)


def _proj_kernel(x_ref, n1_ref, win_ref, qn_ref, wuq_ref, kvn_ref, wuk_ref, ck_ref, sk_ref, cq_ref, sq_ref,
                 ckv_ref, kr_ref, dk_ref, dv_ref, q_ref, dqb_ref, kb_ref, dkb_ref, dvb_ref):
    h = _rms(x_ref[...], n1_ref[...]).astype(BF16)
    z = _dot(h, win_ref[...])
    c_kv = _rms(z[:, _C_KV:_C_KV + D_C], kvn_ref[...])
    ckv_ref[...] = c_kv
    k_rope = z[:, _C_KR:_C_KR + LANES] * ck_ref[...] + z[:, _C_KRR:_C_KRR + LANES] * sk_ref[...]
    kr_ref[...] = k_rope[:, :D_ROPE]
    kb_ref[...] = jnp.concatenate([c_kv, k_rope], axis=1).astype(BF16)
    dk = z[:, _C_DK:_C_DK + MIX_B]
    dk_ref[...] = dk
    dkb_ref[...] = dk.astype(BF16)
    dv = z[:, _C_DV:_C_DV + MIX_B]
    dv_ref[...] = dv
    dvb_ref[...] = dv.astype(BF16)
    dqb_ref[...] = (z[:, _C_DQ:_C_DQ + MIX_B] * SCALE_B).astype(BF16)
    c_q = _rms(z[:, _C_Q:_C_Q + D_CQ], qn_ref[...]).astype(BF16)
    qa = _dot(c_q, wuq_ref[...])
    n_nope = H_A * D_NOPE
    n_rope = H_A * D_ROPE
    q_rope = (qa[:, n_nope:n_nope + n_rope] * cq_ref[...] + qa[:, n_nope + n_rope:] * sq_ref[...]) * SCALE_A
    lane = lax.broadcasted_iota(jnp.int32, (q_rope.shape[0], LANES), 1)
    heads_per_tile = LANES // D_ROPE
    for pair in range(H_A // 2):
        qn2 = qa[:, pair * 2 * D_NOPE:(pair + 1) * 2 * D_NOPE].astype(BF16)
        ql2 = _dot(qn2, wuk_ref[pair]) * SCALE_A
        for j in range(2):
            hd = 2 * pair + j
            tile_r = q_rope[:, (hd // heads_per_tile) * LANES:(hd // heads_per_tile + 1) * LANES]
            shift = (LANES - (hd % heads_per_tile) * D_ROPE) % LANES
            if shift:
                tile_r = pltpu.roll(tile_r, shift, 1)
            qr128 = jnp.where(lane < D_ROPE, tile_r, 0.0)
            q_ref[hd] = jnp.concatenate([ql2[:, j * D_C:(j + 1) * D_C], qr128], axis=1).astype(BF16)


def _proj_call(x2d, tables, n_pos_tiles, w):
    n, d_model = x2d.shape
    r = min(ROW_TILE, n)
    assert n % r == 0
    steps = n // r
    ck, sk, cq, sq = tables
    row = lambda i: (i, 0)
    pos = lambda i: (i % n_pos_tiles, 0)
    const2 = lambda i: (0, 0)
    const3 = lambda i: (0, 0, 0)
    head = lambda i: (0, i, 0)
    in_specs = [
        pl.BlockSpec((r, d_model), row),
        pl.BlockSpec((1, d_model), const2),
        pl.BlockSpec(w["w_in"].shape, const2),
        pl.BlockSpec((1, D_CQ), const2),
        pl.BlockSpec(w["w_uq"].shape, const2),
        pl.BlockSpec((1, D_C), const2),
        pl.BlockSpec(w["w_uk2"].shape, const3),
        pl.BlockSpec((r, LANES), pos),
        pl.BlockSpec((r, LANES), pos),
        pl.BlockSpec((r, H_A * D_ROPE), pos),
        pl.BlockSpec((r, H_A * D_ROPE), pos),
    ]
    out_shape = [
        jax.ShapeDtypeStruct((n, D_C), F32),
        jax.ShapeDtypeStruct((n, D_ROPE), F32),
        jax.ShapeDtypeStruct((n, MIX_B), F32),
        jax.ShapeDtypeStruct((n, MIX_B), F32),
        jax.ShapeDtypeStruct((H_A, n, D_QK), BF16),
        jax.ShapeDtypeStruct((n, MIX_B), BF16),
        jax.ShapeDtypeStruct((n, D_QK), BF16),
        jax.ShapeDtypeStruct((n, MIX_B), BF16),
        jax.ShapeDtypeStruct((n, MIX_B), BF16),
    ]
    out_specs = [
        pl.BlockSpec((r, D_C), row),
        pl.BlockSpec((r, D_ROPE), row),
        pl.BlockSpec((r, MIX_B), row),
        pl.BlockSpec((r, MIX_B), row),
        pl.BlockSpec((H_A, r, D_QK), head),
        pl.BlockSpec((r, MIX_B), row),
        pl.BlockSpec((r, D_QK), row),
        pl.BlockSpec((r, MIX_B), row),
        pl.BlockSpec((r, MIX_B), row),
    ]
    return pl.pallas_call(
        _proj_kernel,
        grid=(steps,),
        in_specs=in_specs,
        out_specs=out_specs,
        out_shape=out_shape,
        compiler_params=pltpu.CompilerParams(dimension_semantics=("arbitrary",), vmem_limit_bytes=VMEM_LIMIT),
        name="proj",
    )(x2d, w["norm1"], w["w_in"], w["q_norm"], w["w_uq"], w["kv_norm"], w["w_uk2"], ck, sk, cq, sq)


def _rep(x, n):
    return jnp.concatenate([x] * n, axis=1) if n > 1 else x


def _softmax_init(s, v, m_ref, l_ref, acc_ref):
    n = s.shape[1] // LANES
    m = jnp.broadcast_to(jnp.max(s, axis=1, keepdims=True), (s.shape[0], LANES))
    p = jnp.exp2(s - _rep(m, n))
    m_ref[...] = m
    psum = p[:, :LANES]
    for j in range(1, n):
        psum = psum + p[:, j * LANES:(j + 1) * LANES]
    l_ref[...] = psum
    acc_ref[...] = _dot(p.astype(BF16), v)


def _softmax_step(s, v, m_ref, l_ref, acc_ref):
    n = s.shape[1] // LANES
    m_prev = m_ref[...]
    m_new = jnp.maximum(m_prev, jnp.max(s, axis=1, keepdims=True))
    alpha = jnp.exp2(m_prev - m_new)
    p = jnp.exp2(s - _rep(m_new, n))
    psum = p[:, :LANES]
    for j in range(1, n):
        psum = psum + p[:, j * LANES:(j + 1) * LANES]
    l_ref[...] = alpha * l_ref[...] + psum
    acc_ref[...] = _rep(alpha, acc_ref.shape[1] // LANES) * acc_ref[...] + _dot(p.astype(BF16), v)
    m_ref[...] = m_new


def _mla_kernel(q_ref, k_ref, km_ref, wuv_ref, o_ref, m_ref, l_ref, acc_ref, *, seq, tile, tk):
    i = pl.program_id(1)
    rows = H_A * tile
    q = q_ref[...].reshape(rows, D_QK)

    def scores(start, size):
        k = k_ref[pl.ds(start, size), :]
        return _dot_nt(q, k), k[:, :D_C]

    km = km_ref[...]
    s, kl = _dot_nt(q, km), km[:, :D_C]
    col = lax.broadcasted_iota(jnp.int32, (rows, LANES), 1)
    _softmax_init(jnp.where(col < N_META, s, NEG_INF), kl, m_ref, l_ref, acc_ref)

    n_far = i * tile
    n_big = n_far // tk

    def big(kt, carry):
        s, kl = scores(pl.multiple_of(kt * tk, tk), tk)
        _softmax_step(s, kl, m_ref, l_ref, acc_ref)
        return carry

    lax.fori_loop(0, n_big, big, 0)

    def small(j, carry):
        s, kl = scores(pl.multiple_of(n_big * tk + j * tile, tile), tile)
        _softmax_step(s, kl, m_ref, l_ref, acc_ref)
        return carry

    lax.fori_loop(0, (n_far - n_big * tk) // tile, small, 0)

    s, kl = scores(pl.multiple_of(i * tile, tile), tile)
    r_in = lax.broadcasted_iota(jnp.int32, (H_A, tile, tile), 1).reshape(rows, tile)
    c_in = lax.broadcasted_iota(jnp.int32, (rows, tile), 1)
    _softmax_step(jnp.where(c_in <= r_in, s, NEG_INF), kl, m_ref, l_ref, acc_ref)

    o_lat = (acc_ref[...] / jnp.sum(l_ref[...], axis=1, keepdims=True)).astype(BF16)
    o_ref[...] = jnp.concatenate(
        [_dot(o_lat[hd * tile:(hd + 1) * tile], wuv_ref[hd]) for hd in range(H_A)], axis=1).astype(BF16)


def _mla_call(q, k, k_meta, wuv, seq):
    b = k.shape[0]
    tile = ATT_TQ
    tk = ATT_TK
    assert seq % tile == 0 and tile >= MAX_DISTANCE and tk % tile == 0
    nk = k.shape[1]
    rows = H_A * tile
    nq = seq // tile
    return pl.pallas_call(
        functools.partial(_mla_kernel, seq=seq, tile=tile, tk=tk),
        grid=(b, nq),
        in_specs=[
            pl.BlockSpec((H_A, tile, D_QK), lambda bi, i: (0, bi * nq + i, 0)),
            pl.BlockSpec((None, nk, D_QK), lambda bi, i: (bi, 0, 0)),
            pl.BlockSpec((LANES, D_QK), lambda bi, i: (0, 0)),
            pl.BlockSpec(wuv.shape, lambda bi, i: (0, 0, 0)),
        ],
        out_specs=pl.BlockSpec((None, tile, MIX_A), lambda bi, i: (bi, i, 0)),
        out_shape=jax.ShapeDtypeStruct((b, seq, MIX_A), BF16),
        scratch_shapes=[pltpu.VMEM((rows, LANES), F32), pltpu.VMEM((rows, LANES), F32), pltpu.VMEM((rows, D_C), F32)],
        compiler_params=pltpu.CompilerParams(dimension_semantics=("arbitrary", "arbitrary"),
                                             vmem_limit_bytes=VMEM_LIMIT),
        name="mla",
    )(q, k, k_meta, wuv)


def _lambda_value(lq1_ref, lk1_ref, lq2_ref, lk2_ref, lam_init):
    s1 = jnp.sum(lq1_ref[...] * lk1_ref[...], axis=1, keepdims=True)
    s2 = jnp.sum(lq2_ref[...] * lk2_ref[...], axis=1, keepdims=True)
    return jnp.exp(s1) - jnp.exp(s2) + lam_init


def _bias_from_buckets(bk, relb_ref, hd):
    far = relb_ref[N_BUCKETS - 1, hd]
    bias = jnp.zeros(bk.shape, F32)
    for bkt in range(N_BUCKETS - 1):
        bias = jnp.where(bk == bkt, (relb_ref[bkt, hd] - far) * LOG2E, bias)
    return bias


def _diff_kernel(relb_ref, q_ref, k_ref, v_ref, km_ref, vm_ref, bkd_ref, bkp_ref, bkm_ref, lq1_ref, lk1_ref, lq2_ref, lk2_ref,
                 subln_ref, o_ref, bd_ref, bp_ref, bm_ref, m_ref, l_ref, acc_ref, *, seq, tile, tk, lam_init):
    hd = pl.program_id(1)
    i = pl.program_id(2)
    width = 2 * D_B

    @pl.when(i == 0)
    def _():
        bd = _bias_from_buckets(bkd_ref[...], relb_ref, hd)
        bd_ref[...] = jnp.concatenate([bd, bd], axis=0)
        bp = _bias_from_buckets(bkp_ref[...], relb_ref, hd)
        bp_ref[...] = jnp.concatenate([bp, bp], axis=0)
        bm = _bias_from_buckets(bkm_ref[...], relb_ref, hd)
        bm_ref[...] = jnp.concatenate([bm, bm], axis=0)

    q = q_ref[...]
    lane = lax.broadcasted_iota(jnp.int32, (tile, width), 1)
    zero = jnp.zeros_like(q)
    qbd = jnp.concatenate([jnp.where(lane < D_B, q, zero), jnp.where(lane >= D_B, q, zero)], axis=0)

    def scores(start, size):
        return _dot_nt(qbd, k_ref[pl.ds(start, size), :]), v_ref[pl.ds(start, size), :]

    s, v = _dot_nt(qbd, km_ref[...]), vm_ref[...]
    s = s + jnp.where(i == 0, 1.0, 0.0) * bm_ref[...]
    col = lax.broadcasted_iota(jnp.int32, (2 * tile, LANES), 1)
    _softmax_init(jnp.where(col < N_META, s, NEG_INF), v, m_ref, l_ref, acc_ref)

    n_far = jnp.maximum(i - 1, 0) * tile
    n_big = n_far // tk

    def big(kt, carry):
        s, v = scores(pl.multiple_of(kt * tk, tk), tk)
        _softmax_step(s, v, m_ref, l_ref, acc_ref)
        return carry

    lax.fori_loop(0, n_big, big, 0)

    def small(j, carry):
        s, v = scores(pl.multiple_of(n_big * tk + j * tile, tile), tile)
        _softmax_step(s, v, m_ref, l_ref, acc_ref)
        return carry

    lax.fori_loop(0, (n_far - n_big * tk) // tile, small, 0)

    @pl.when(i > 0)
    def _():
        s, v = scores(pl.multiple_of((i - 1) * tile, tile), tile)
        _softmax_step(s + bp_ref[...], v, m_ref, l_ref, acc_ref)

    s, v = scores(pl.multiple_of(i * tile, tile), tile)
    r_in = lax.broadcasted_iota(jnp.int32, (2, tile, tile), 1).reshape(2 * tile, tile)
    c_in = lax.broadcasted_iota(jnp.int32, (2 * tile, tile), 1)
    _softmax_step(jnp.where(c_in <= r_in, s + bd_ref[...], NEG_INF), v, m_ref, l_ref, acc_ref)

    o = acc_ref[...] / jnp.sum(l_ref[...], axis=1, keepdims=True)
    lam = _lambda_value(lq1_ref, lk1_ref, lq2_ref, lk2_ref, lam_init)
    a = o[:tile] - lam * o[tile:]
    o_ref[...] = (_rms(a, subln_ref[...]) * (1.0 - lam_init)).astype(BF16)


def _diff_call(relb, dq, dk, dv, dk_meta, dv_meta, lam_w, subln, seq, lam_init):
    b = dq.shape[0]
    tile = DIFF_TQ
    tk = DIFF_TK
    assert seq % tile == 0 and tile >= MAX_DISTANCE and tk % tile == 0
    nk = dk.shape[1]
    width = 2 * D_B
    r = np.arange(tile)[:, None]
    c = np.arange(tile)[None, :]
    bk_diag = jnp.asarray(_bucket_np(r - c))
    bk_prev = jnp.asarray(_bucket_np(tile + r - c))
    bk_meta = jnp.asarray(_bucket_np(N_META + r - np.arange(LANES)[None, :]))
    const = lambda bi, h, i: (0, 0)
    smem = pl.BlockSpec(memory_space=pltpu.SMEM)
    return pl.pallas_call(
        functools.partial(_diff_kernel, seq=seq, tile=tile, tk=tk, lam_init=lam_init),
        grid=(b, H_B, seq // tile),
        in_specs=[
            smem,
            pl.BlockSpec((None, tile, width), lambda bi, h, i: (bi, i, h)),
            pl.BlockSpec((None, nk, width), lambda bi, h, i: (bi, 0, h)),
            pl.BlockSpec((None, nk, width), lambda bi, h, i: (bi, 0, h)),
            pl.BlockSpec((LANES, width), lambda bi, h, i: (0, h)),
            pl.BlockSpec((LANES, width), lambda bi, h, i: (0, h)),
            pl.BlockSpec((tile, tile), const),
            pl.BlockSpec((tile, tile), const),
            pl.BlockSpec((tile, LANES), const),
            pl.BlockSpec((1, LANES), const),
            pl.BlockSpec((1, LANES), const),
            pl.BlockSpec((1, LANES), const),
            pl.BlockSpec((1, LANES), const),
            pl.BlockSpec((1, width), const),
        ],
        out_specs=pl.BlockSpec((None, tile, width), lambda bi, h, i: (bi, i, h)),
        out_shape=jax.ShapeDtypeStruct((b, seq, MIX_B), BF16),
        scratch_shapes=[
            pltpu.VMEM((2 * tile, tile), F32),
            pltpu.VMEM((2 * tile, tile), F32),
            pltpu.VMEM((2 * tile, LANES), F32),
            pltpu.VMEM((2 * tile, LANES), F32),
            pltpu.VMEM((2 * tile, LANES), F32),
            pltpu.VMEM((2 * tile, width), F32),
        ],
        compiler_params=pltpu.CompilerParams(dimension_semantics=("arbitrary", "arbitrary", "arbitrary"),
                                             vmem_limit_bytes=VMEM_LIMIT),
        name="diff",
    )(relb, dq, dk, dv, dk_meta, dv_meta, bk_diag, bk_prev, bk_meta, *lam_w, subln)


def _decode_kernel(pt_ref, *refs, n_pages, group, lam_init):
    del pt_ref
    page_refs = refs[:4 * group]
    (q_ref, dq_ref, kself_ref, dk_ref, dv_ref, relb_ref, bkl_ref, wuv_ref,
     lq1_ref, lk1_ref, lq2_ref, lk2_ref, subln_ref,
     oa_ref, ob_ref, ma_ref, la_ref, acca_ref, md_ref, ld_ref, accd_ref) = refs[4 * group:]
    lat_refs = page_refs[0::4]
    krt_refs = page_refs[1::4]
    dkt_refs = page_refs[2::4]
    dvp_refs = page_refs[3::4]
    c = pl.program_id(1)
    n_steps = n_pages // group
    n_maps = 2 * H_B

    q = q_ref[...]
    ql = q[:, :D_C]
    qr = q[:, D_C:D_C + D_ROPE]
    row = lax.broadcasted_iota(jnp.int32, (n_maps, MIX_B), 0)
    lane_blk = lax.broadcasted_iota(jnp.int32, (n_maps, MIX_B), 1) // D_B
    dqbd = jnp.where(row == lane_blk, jnp.broadcast_to(dq_ref[...], (n_maps, MIX_B)), 0.0)
    relb = relb_ref[...]
    far = relb[:, N_BUCKETS - 1:N_BUCKETS]

    @pl.when(c == 0)
    def _():
        k_self = kself_ref[...]
        ckv = k_self[:, :D_C]
        lane0 = jnp.where(lax.broadcasted_iota(jnp.int32, (H_A, LANES), 1) == 0, 1.0, 0.0)
        ma_ref[...] = jnp.broadcast_to(jnp.sum(q * k_self, axis=1, keepdims=True), (H_A, LANES))
        la_ref[...] = lane0
        acca_ref[...] = jnp.broadcast_to(ckv, (H_A, D_C))
        md_ref[...] = jnp.broadcast_to(
            jnp.sum(dqbd * dk_ref[...], axis=1, keepdims=True) + (relb[:, 0:1] - far) * LOG2E, (n_maps, LANES))
        ld_ref[...] = lane0
        dv = dv_ref[...]
        accd_ref[...] = jnp.concatenate(
            [jnp.broadcast_to(dv[:, (r // 2) * 2 * D_B:(r // 2 + 1) * 2 * D_B], (1, 2 * D_B)) for r in range(n_maps)], axis=0)

    s_a = [_dot_nt(ql, lat_refs[g][...]) + _dot(qr, krt_refs[g][...]) for g in range(group)]
    m_prev = ma_ref[...]
    m_lane = s_a[0]
    for s in s_a[1:]:
        m_lane = jnp.maximum(m_lane, s)
    m_new = jnp.maximum(m_prev, jnp.max(m_lane, axis=1, keepdims=True))
    alpha = jnp.exp2(m_prev - m_new)
    l_new = alpha * la_ref[...]
    pv = None
    for g in range(group):
        p = jnp.exp2(s_a[g] - m_new)
        l_new = l_new + p
        part = _dot(p, lat_refs[g][...])
        pv = part if pv is None else pv + part
    ma_ref[...] = m_new
    la_ref[...] = l_new
    acca_ref[...] = _rep(alpha, D_C // LANES) * acca_ref[...] + pv

    bkl = bkl_ref[...]
    bias_last = jnp.zeros((n_maps, bkl.shape[1]), F32)
    for bkt in range(N_BUCKETS - 1):
        bias_last = jnp.where(bkl == bkt, (relb[:, bkt:bkt + 1] - far) * LOG2E, bias_last)
    s_d = [_dot(dqbd, dkt_refs[g][...]) for g in range(group)]
    s_d[group - 1] = s_d[group - 1] + jnp.where(c == n_steps - 1, 1.0, 0.0) * bias_last
    m_prev = md_ref[...]
    m_lane = s_d[0]
    for s in s_d[1:]:
        m_lane = jnp.maximum(m_lane, s)
    m_new = jnp.maximum(m_prev, jnp.max(m_lane, axis=1, keepdims=True))
    alpha = jnp.exp2(m_prev - m_new)
    l_new = alpha * ld_ref[...]
    page = bkl.shape[1]
    pv = None
    for g in range(group):
        p = jnp.exp2(s_d[g] - m_new)
        l_new = l_new + p
        part = jnp.concatenate(
            [_dot(p[2 * hd:2 * hd + 2], dvp_refs[g][pl.ds(hd, page, stride=H_B), :]) for hd in range(H_B)], axis=0)
        pv = part if pv is None else pv + part
    md_ref[...] = m_new
    ld_ref[...] = l_new
    accd_ref[...] = alpha * accd_ref[...] + pv

    @pl.when(c == n_steps - 1)
    def _():
        o_lat = acca_ref[...] / jnp.sum(la_ref[...], axis=1, keepdims=True)
        oa_ref[...] = jnp.concatenate(
            [_dot(o_lat[hd:hd + 1].astype(BF16), wuv_ref[hd]) for hd in range(H_A)], axis=1).astype(BF16)
        o = accd_ref[...] / jnp.sum(ld_ref[...], axis=1, keepdims=True)
        lam = _lambda_value(lq1_ref, lk1_ref, lq2_ref, lk2_ref, lam_init)
        parts = []
        for hd in range(H_B):
            a = o[2 * hd:2 * hd + 1] - lam * o[2 * hd + 1:2 * hd + 2]
            parts.append(_rms(a, subln_ref[...]) * (1.0 - lam_init))
        ob_ref[...] = jnp.concatenate(parts, axis=1).astype(BF16)


def _decode_call(page_table, lat, krt, dkt, dvp, q, dq, k_self, dk, dv, relb8, wuv, lam_w, subln, lam_init):
    n_batch, n_pages = page_table.shape
    page = lat.shape[1]
    group = min(PAGES_PER_STEP, n_pages)
    assert n_pages % group == 0 and page >= MAX_DISTANCE
    n_maps = 2 * H_B
    bk_last = jnp.asarray(_bucket_np(page - np.arange(page))[None, :])

    def page_map(g):
        return lambda bi, c, pt: (pt[bi * n_pages + c * group + g], 0, 0)

    in_specs = []
    operands = []
    for g in range(group):
        in_specs += [
            pl.BlockSpec((None, page, D_C), page_map(g)),
            pl.BlockSpec((None, D_ROPE, page), page_map(g)),
            pl.BlockSpec((None, MIX_B, page), page_map(g)),
            pl.BlockSpec((None, H_B * page, 2 * D_B), page_map(g)),
        ]
        operands += [lat, krt, dkt, dvp]
    per_b = lambda bi, c, pt: (bi, 0, 0)
    const2 = lambda bi, c, pt: (0, 0)
    const3 = lambda bi, c, pt: (0, 0, 0)
    in_specs += [
        pl.BlockSpec((None, H_A, D_QK), per_b),
        pl.BlockSpec((None, 1, MIX_B), per_b),
        pl.BlockSpec((None, 1, D_QK), per_b),
        pl.BlockSpec((None, 1, MIX_B), per_b),
        pl.BlockSpec((None, 1, MIX_B), per_b),
        pl.BlockSpec((n_maps, N_BUCKETS), const2),
        pl.BlockSpec((1, page), const2),
        pl.BlockSpec(wuv.shape, const3),
        pl.BlockSpec((1, LANES), const2),
        pl.BlockSpec((1, LANES), const2),
        pl.BlockSpec((1, LANES), const2),
        pl.BlockSpec((1, LANES), const2),
        pl.BlockSpec((1, 2 * D_B), const2),
    ]
    operands += [q, dq, k_self, dk, dv, relb8, bk_last, wuv, *lam_w, subln]
    grid_spec = pltpu.PrefetchScalarGridSpec(
        num_scalar_prefetch=1,
        grid=(n_batch, n_pages // group),
        in_specs=in_specs,
        out_specs=[pl.BlockSpec((None, 1, MIX_A), per_b), pl.BlockSpec((None, 1, MIX_B), per_b)],
        scratch_shapes=[
            pltpu.VMEM((H_A, LANES), F32), pltpu.VMEM((H_A, LANES), F32), pltpu.VMEM((H_A, D_C), F32),
            pltpu.VMEM((n_maps, LANES), F32), pltpu.VMEM((n_maps, LANES), F32), pltpu.VMEM((n_maps, 2 * D_B), F32),
        ],
    )
    return pl.pallas_call(
        functools.partial(_decode_kernel, n_pages=n_pages, group=group, lam_init=lam_init),
        grid_spec=grid_spec,
        out_shape=[jax.ShapeDtypeStruct((n_batch, 1, MIX_A), BF16), jax.ShapeDtypeStruct((n_batch, 1, MIX_B), BF16)],
        compiler_params=pltpu.CompilerParams(dimension_semantics=("arbitrary", "arbitrary"),
                                             vmem_limit_bytes=VMEM_LIMIT),
        name="decode",
    )(page_table.reshape(-1), *operands)


def _post_kernel(x_ref, oa_ref, ob_ref, wo_ref, n2_ref, wup_ref, wdn_ref, fn_ref, y_ref, *, ff_chunk):
    attn = _dot(oa_ref[...], wo_ref[:MIX_A, :]) + _dot(ob_ref[...], wo_ref[MIX_A:, :])
    x1 = x_ref[...] + attn
    h2 = _rms(x1, n2_ref[...]).astype(BF16)
    d_ff = wup_ref.shape[1]
    mlp = None
    for j in range(d_ff // ff_chunk):
        u = _dot(h2, wup_ref[:, j * ff_chunk:(j + 1) * ff_chunk])
        g = jnp.square(jnp.maximum(u, 0.0)).astype(BF16)
        part = _dot(g, wdn_ref[j * ff_chunk:(j + 1) * ff_chunk, :])
        mlp = part if mlp is None else mlp + part
    y_ref[...] = _rms(x1 + mlp, fn_ref[...])


def _post_call(x2d, oa, ob, w):
    n, d_model = x2d.shape
    r = min(ROW_TILE, n)
    assert n % r == 0
    d_ff = w["w_up"].shape[1]
    row = lambda i: (i, 0)
    const = lambda i: (0, 0)
    return pl.pallas_call(
        functools.partial(_post_kernel, ff_chunk=min(1024, d_ff)),
        grid=(n // r,),
        in_specs=[
            pl.BlockSpec((r, d_model), row),
            pl.BlockSpec((r, MIX_A), row),
            pl.BlockSpec((r, MIX_B), row),
            pl.BlockSpec(w["w_o"].shape, const),
            pl.BlockSpec((1, d_model), const),
            pl.BlockSpec(w["w_up"].shape, const),
            pl.BlockSpec(w["w_down"].shape, const),
            pl.BlockSpec((1, d_model), const),
        ],
        out_specs=pl.BlockSpec((r, d_model), row),
        out_shape=jax.ShapeDtypeStruct((n, d_model), F32),
        compiler_params=pltpu.CompilerParams(dimension_semantics=("arbitrary",), vmem_limit_bytes=VMEM_LIMIT),
        name="post",
    )(x2d, oa, ob, w["w_o"], w["norm2"], w["w_up"], w["w_down"], w["final_norm"])


def _rope_tables(pos):
    half = D_ROPE // 2
    inv = ROPE_BASE ** (-jnp.arange(half, dtype=F32) / half)
    ang = pos.astype(F32)[:, None] * inv[None, :]
    cos, sin = jnp.cos(ang), jnp.sin(ang)
    ck = jnp.concatenate([cos, cos], axis=1)
    sk = jnp.concatenate([-sin, sin], axis=1)
    lane_pad = ((0, 0), (0, LANES - D_ROPE))
    return jnp.pad(ck, lane_pad), jnp.pad(sk, lane_pad), jnp.tile(ck, (1, H_A)), jnp.tile(sk, (1, H_A))


def _prep_weights(norm1, w_in, q_norm, w_uq, kv_norm, w_uk, w_uv, w_o, norm2, w_up, w_down, final_norm):
    d_model = w_in.shape[0]
    half = D_ROPE // 2
    cuts = np.cumsum([0, D_CQ, D_C, D_ROPE, MIX_B, MIX_B, MIX_B])
    c_q, c_kv, k_r, dq, dk, dv = (w_in[:, cuts[j]:cuts[j + 1]] for j in range(6))
    k_rr = jnp.concatenate([k_r[:, half:], k_r[:, :half]], axis=1)
    pad = jnp.zeros((d_model, LANES - D_ROPE), w_in.dtype)
    w_in_p = jnp.concatenate([c_q, c_kv, dq, dk, dv, k_r, pad, k_rr, pad], axis=1).astype(BF16)
    assert w_in_p.shape[1] == _IN_COLS_P
    wq = w_uq.reshape(D_CQ, H_A, D_NOPE + D_ROPE)
    nope = wq[:, :, :D_NOPE].reshape(D_CQ, H_A * D_NOPE)
    rope = wq[:, :, D_NOPE:]
    rope_rot = jnp.concatenate([rope[:, :, half:], rope[:, :, :half]], axis=2)
    w_uq_p = jnp.concatenate([nope, rope.reshape(D_CQ, -1), rope_rot.reshape(D_CQ, -1)], axis=1).astype(BF16)
    wk = jnp.transpose(w_uk, (1, 2, 0))
    zero = jnp.zeros((D_NOPE, D_C), w_uk.dtype)
    w_uk2 = jnp.stack([
        jnp.concatenate([jnp.concatenate([wk[2 * j], zero], axis=1), jnp.concatenate([zero, wk[2 * j + 1]], axis=1)], axis=0)
        for j in range(H_A // 2)]).astype(BF16)
    return {
        "norm1": norm1.reshape(1, -1), "w_in": w_in_p, "q_norm": q_norm.reshape(1, -1), "w_uq": w_uq_p,
        "kv_norm": kv_norm.reshape(1, -1), "w_uk2": w_uk2,
        "w_uv": jnp.transpose(w_uv, (1, 0, 2)).astype(BF16),
        "w_o": w_o.astype(BF16), "norm2": norm2.reshape(1, -1), "w_up": w_up.astype(BF16),
        "w_down": w_down.astype(BF16), "final_norm": final_norm.reshape(1, -1),
    }


def kernel(x_prompt, x_sample, cache_latent, cache_krope, cache_diff_k, cache_diff_v, page_table, meta_tokens, rel_bias,
           norm1, w_in, q_norm, w_uq, kv_norm, w_uk, w_uv, lambda_q1, lambda_k1, lambda_q2, lambda_k2, subln, w_o, norm2,
           w_up, w_down, final_norm):
    depth = norm1.shape[0]
    assert depth == 1, "single-layer trunk"
    layer = 0
    lam_init = 0.8 - 0.6 * math.exp(-0.3 * layer)
    b, seq, d_model = x_prompt.shape
    n_dec, dec_seq, _ = x_sample.shape
    assert dec_seq == 1
    n_phys, page = cache_latent.shape[1], cache_latent.shape[2]
    past_len = page_table.shape[1] * page

    w = _prep_weights(norm1[layer], w_in[layer], q_norm[layer], w_uq[layer], kv_norm[layer], w_uk[layer], w_uv[layer],
                      w_o[layer], norm2[layer], w_up[layer], w_down[layer], final_norm)
    lam_w = [jnp.pad(v[layer].reshape(1, -1), ((0, 0), (0, LANES - D_B)))
             for v in (lambda_q1, lambda_k1, lambda_q2, lambda_k2)]
    subln2 = subln[layer].reshape(1, -1)

    r_p = min(ROW_TILE, seq)
    tab_p = _rope_tables(N_META + jnp.arange(seq, dtype=jnp.int32))
    p_out = _proj_call(x_prompt.reshape(b * seq, d_model), tab_p, seq // r_p, w)
    tab_m = _rope_tables(jnp.arange(N_META, dtype=jnp.int32))
    m_out = _proj_call(meta_tokens.astype(x_prompt.dtype), tab_m, 1, w)
    tab_s = _rope_tables(jnp.full((n_dec,), past_len, dtype=jnp.int32))
    s_out = _proj_call(x_sample.reshape(n_dec, d_model), tab_s, 1, w)

    def keys(p_arr):
        return p_arr.reshape(b, seq, p_arr.shape[-1])

    def meta(m_arr):
        return jnp.pad(m_arr, ((0, LANES - N_META), (0, 0)))

    (p_ckv, p_kr, p_dk, p_dv, p_q, p_dqb, p_kb, p_dkb, p_dvb) = p_out
    (m_ckv, m_kr, m_dk, m_dv, _, _, m_kb, m_dkb, m_dvb) = m_out
    o_a = _mla_call(p_q, keys(p_kb), meta(m_kb), w["w_uv"], seq)
    o_b = _diff_call(rel_bias, p_dqb.reshape(b, seq, MIX_B), keys(p_dkb), keys(p_dvb), meta(m_dkb), meta(m_dvb),
                     lam_w, subln2, seq, lam_init)
    y_prompt = _post_call(x_prompt.reshape(b * seq, d_model), o_a.reshape(b * seq, MIX_A), o_b.reshape(b * seq, MIX_B), w)

    (s_ckv, s_kr, s_dk, s_dv, s_q, s_dqb, _, _, _) = s_out
    k_self = jnp.concatenate([s_ckv, jnp.pad(s_kr, ((0, 0), (0, LANES - D_ROPE)))], axis=1)
    lat = cache_latent[layer]
    krt = jnp.transpose(cache_krope[layer], (0, 2, 1))
    dkt = jnp.transpose(cache_diff_k[layer], (0, 2, 3, 4, 1)).reshape(n_phys, MIX_B, page)
    dvp = cache_diff_v[layer].reshape(n_phys, page * H_B, 2 * D_B)
    relb8 = jnp.repeat(rel_bias.T, 2, axis=0)
    o_a_s, o_b_s = _decode_call(
        page_table, lat, krt, dkt, dvp,
        jnp.transpose(s_q, (1, 0, 2)).astype(F32), s_dqb.astype(F32).reshape(n_dec, 1, MIX_B),
        k_self.reshape(n_dec, 1, D_QK), s_dk.reshape(n_dec, 1, MIX_B), s_dv.reshape(n_dec, 1, MIX_B),
        relb8, w["w_uv"], lam_w, subln2, lam_init)
    y_sample = _post_call(x_sample.reshape(n_dec, d_model), o_a_s.reshape(n_dec, MIX_A), o_b_s.reshape(n_dec, MIX_B), w)

    def rows_p(m_arr, p_arr):
        width = p_arr.shape[-1]
        return jnp.concatenate([jnp.broadcast_to(m_arr[None], (b, N_META, width)), p_arr.reshape(b, seq, width)], axis=1)[None]

    t_all = N_META + seq
    return (
        y_prompt.reshape(b, seq, d_model),
        y_sample.reshape(n_dec, 1, d_model),
        rows_p(m_ckv, p_ckv),
        rows_p(m_kr, p_kr),
        rows_p(m_dk, p_dk).reshape(1, b, t_all, H_B, 2, D_B),
        rows_p(m_dv, p_dv).reshape(1, b, t_all, H_B, 2 * D_B),
        s_ckv.reshape(1, n_dec, 1, D_C),
        s_kr.reshape(1, n_dec, 1, D_ROPE),
        s_dk.reshape(1, n_dec, 1, H_B, 2, D_B),
        s_dv.reshape(1, n_dec, 1, H_B, 2 * D_B),
    )
```
